```python
import jax, jax.numpy as jnp
from jax import lax
import numpy as np

D_MODEL = 1024
BATCH = 32
SEQ = 256
DEPTH = 4
DEC_BATCH = 8
DEC_SEQ = 4096
PAST_LEN = 256

GRID_W = 64
N_MIXERS = 3
BLOCK = 128
ROPE_THETA = 10000.0
EPS = 1e-6
NEG_INF = -1e30
ALPHA = (2 * DEPTH) ** 0.25
BETA = (8 * DEPTH) ** -0.25
N_MOD = 6

A_HEADS = 8
A_NOPE = 128
A_ROPE = 64
A_V = 128
A_Q_LORA = 512
A_KV_LORA = 256
B_HEADS = 16
B_KV = 4
B_HD = 64
WINDOW = 128
C_HEADS = 8
C_KV = 4
C_HD = 128
P_HEADS = 8
P_NKEYS = 128
P_EXPERTS = P_NKEYS * P_NKEYS
P_DKEY = 128
P_TOPK = 16

N_A = (DEPTH + 2) // 3
N_B = (DEPTH + 1) // 3
N_C = DEPTH // 3

kernel_name = "hybrid_mla_swa_axialgqa_peer_diffusion_step"


def rms_norm(x, g):
    xf = x.astype(jnp.float32)
    y = xf * lax.rsqrt(jnp.mean(xf * xf, -1, keepdims=True) + EPS)
    return (y * g).astype(x.dtype)


def layer_norm(x, g, b):
    xf = x.astype(jnp.float32)
    mu = jnp.mean(xf, -1, keepdims=True)
    var = jnp.mean(jnp.square(xf - mu), -1, keepdims=True)
    return ((xf - mu) * lax.rsqrt(var + EPS) * g + b).astype(x.dtype)


def modulate(x, shift, scale):
    return x * (1 + scale[:, None]) + shift[:, None]


def axial_angles(n, rot_dim):
    n_rows = n // GRID_W
    rows = jnp.repeat(jnp.arange(n_rows, dtype=jnp.float32), GRID_W)
    cols = jnp.tile(jnp.arange(GRID_W, dtype=jnp.float32), n_rows)
    quarter = rot_dim // 4
    freqs = ROPE_THETA ** (-jnp.arange(quarter, dtype=jnp.float32) / quarter)
    ang = jnp.concatenate([rows[:, None] * freqs, cols[:, None] * freqs], -1)
    return jnp.cos(ang), jnp.sin(ang)


def apply_rope(x, cs):
    cos, sin = cs
    rd = x.shape[-1]
    qd = rd // 4
    xs = x.reshape(*x.shape[:-1], 2, 2, qd)
    x1, x2 = xs[..., 0, :], xs[..., 1, :]
    bshape = (x.shape[1],) + (1,) * (x.ndim - 3) + (2, qd)
    c = cos.reshape(bshape)
    s = sin.reshape(bshape)
    out = jnp.stack([x1 * c - x2 * s, x1 * s + x2 * c], axis=-2)
    return out.reshape(x.shape).astype(x.dtype)


def sink_softmax(s, sink):
    if sink is None:
        return jax.nn.softmax(s, axis=-1)
    sk = sink.astype(jnp.float32).reshape(1, s.shape[1], s.shape[2], 1, 1)
    m = jnp.maximum(jnp.max(s, -1, keepdims=True), sk)
    e = jnp.exp(s - m)
    return e / (jnp.sum(e, -1, keepdims=True) + jnp.exp(sk - m))


def joint_attention(q_ctx, k_ctx, v_ctx, q_lat=None, k_lat=None, v_lat=None, sink=None):
    B_, S, G, R, dk = q_ctx.shape
    nb = S // BLOCK
    scale = dk ** -0.5
    has_lat = q_lat is not None
    v_all = jnp.concatenate([v_ctx, v_lat], axis=1) if has_lat else v_ctx

    def blocks(a):
        return jnp.moveaxis(a.reshape(B_, nb, BLOCK, G, R, dk), 1, 0)

    qs = (blocks(q_ctx), blocks(q_lat)) if has_lat else (blocks(q_ctx),)

    def one_block(qb):
        s = jnp.einsum('bqgrd,bkgd->bgrqk', qb[0], k_ctx)
        if has_lat:
            s = jnp.concatenate([s, jnp.einsum('bqgrd,bkgd->bgrqk', qb[1], k_lat)], -1)
        p = sink_softmax(s.astype(jnp.float32) * scale, sink)
        return jnp.einsum('bgrqk,bkgd->bqgrd', p.astype(v_all.dtype), v_all)

    out = lax.map(one_block, qs)
    return jnp.moveaxis(out, 0, 1).reshape(B_, S, G * R * v_all.shape[-1])


def banded_attention(q_lat, k_lat, v_lat, q_ctx, k_ctx, v_ctx, sink):
    B_, S, G, R, dk = q_lat.shape
    T = k_ctx.shape[1]
    nb = S // BLOCK
    span = BLOCK + 2 * WINDOW
    scale = dk ** -0.5
    pad = ((0, 0), (WINDOW, WINDOW), (0, 0), (0, 0))
    kp = jnp.pad(k_lat, pad)
    vp = jnp.pad(v_lat, pad)
    offs_q = jnp.arange(BLOCK)
    offs_k = jnp.arange(span) - WINDOW

    def blocks(a):
        return jnp.moveaxis(a.reshape(B_, nb, BLOCK, G, R, dk), 1, 0)

    def one_block(args):
        i, ql, qc = args
        start = i * BLOCK
        kb = lax.dynamic_slice_in_dim(kp, start, span, axis=1)
        vb = lax.dynamic_slice_in_dim(vp, start, span, axis=1)
        kpos = start + offs_k
        qpos = start + offs_q
        valid = (jnp.abs(qpos[:, None] - kpos[None, :]) <= WINDOW) & (kpos[None, :] >= 0) & (kpos[None, :] < S)
        s_c = jnp.einsum('bqgrd,bkgd->bgrqk', qc, k_ctx).astype(jnp.float32) * scale
        s_l = jnp.einsum('bqgrd,bkgd->bgrqk', ql, kb).astype(jnp.float32) * scale
        s_l = jnp.where(valid, s_l, NEG_INF)
        p = sink_softmax(jnp.concatenate([s_c, s_l], -1), sink).astype(v_lat.dtype)
        return (jnp.einsum('bgrqk,bkgd->bqgrd', p[..., :T], v_ctx)
                + jnp.einsum('bgrqk,bkgd->bqgrd', p[..., T:], vb))

    out = lax.map(one_block, (jnp.arange(nb), blocks(q_lat), blocks(q_ctx)))
    return jnp.moveaxis(out, 0, 1).reshape(B_, S, G * R * v_lat.shape[-1])


def mla_mixer(h_c, h_l, cache_lat, cs, w_in, q_gain, kv_gain, w_uq, w_ukv, w_o):
    def compress(h):
        B_, S, _ = h.shape
        p = h @ w_in
        cq = rms_norm(p[..., :A_Q_LORA], q_gain)
        lat = jnp.concatenate([rms_norm(p[..., A_Q_LORA:A_Q_LORA + A_KV_LORA], kv_gain),
                               p[..., A_Q_LORA + A_KV_LORA:]], -1)
        q = (cq @ w_uq).reshape(B_, S, A_HEADS, A_NOPE + A_ROPE)
        return q, lat

    def expand(lat, rope_cs):
        B_, T, _ = lat.shape
        kv = (lat[..., :A_KV_LORA] @ w_ukv).reshape(B_, T, A_HEADS, A_NOPE + A_V)
        kr = lat[..., None, A_KV_LORA:]
        if rope_cs is not None:
            kr = apply_rope(kr, rope_cs)
        k = jnp.concatenate([kv[..., :A_NOPE], jnp.broadcast_to(kr, (B_, T, A_HEADS, A_ROPE))], -1)
        return k, kv[..., A_NOPE:]

    q_c, lat_c = compress(h_c)
    k_c, v_c = expand(lat_c, None)
    y_c = joint_attention(q_c[:, :, :, None], k_c, v_c) @ w_o
    q_l, lat_l = compress(h_l)
    q_rot = jnp.concatenate([q_l[..., :A_NOPE], apply_rope(q_l[..., A_NOPE:], cs)], -1)
    k_l, v_l = expand(lat_l, cs)
    k_cc, v_cc = expand(cache_lat, None)
    y_l = joint_attention(q_l[:, :, :, None], k_cc, v_cc, q_rot[:, :, :, None], k_l, v_l) @ w_o
    return y_c, y_l, lat_c


def swa_mixer(h_c, h_l, k_cache, v_cache, cs, w_qkv, sink, w_o):
    def proj(h):
        B_, S, _ = h.shape
        p = h @ w_qkv
        q = p[..., :B_HEADS * B_HD].reshape(B_, S, B_KV, B_HEADS // B_KV, B_HD)
        k = p[..., B_HEADS * B_HD:(B_HEADS + B_KV) * B_HD].reshape(B_, S, B_KV, B_HD)
        v = p[..., (B_HEADS + B_KV) * B_HD:].reshape(B_, S, B_KV, B_HD)
        return q, k, v

    q_c, k_c, v_c = proj(h_c)
    y_c = joint_attention(q_c, k_c, v_c, sink=sink) @ w_o
    q_l, k_l, v_l = proj(h_l)
    y_l = banded_attention(apply_rope(q_l, cs), apply_rope(k_l, cs), v_l, q_l, k_cache, v_cache, sink) @ w_o
    return y_c, y_l, k_c, v_c


def gqa_mixer(h_c, h_l, k_cache, v_cache, cs, w_qkv, q_gain, k_gain, w_o):
    def proj(h):
        B_, S, _ = h.shape
        p = h @ w_qkv
        q = rms_norm(p[..., :C_HEADS * C_HD].reshape(B_, S, C_KV, C_HEADS // C_KV, C_HD), q_gain)
        k = rms_norm(p[..., C_HEADS * C_HD:(C_HEADS + C_KV) * C_HD].reshape(B_, S, C_KV, C_HD), k_gain)
        v = p[..., (C_HEADS + C_KV) * C_HD:].reshape(B_, S, C_KV, C_HD)
        return q, k, v

    q_c, k_c, v_c = proj(h_c)
    y_c = joint_attention(q_c, k_c, v_c) @ w_o
    q_l, k_l, v_l = proj(h_l)
    y_l = joint_attention(q_l, k_cache, v_cache, apply_rope(q_l, cs), apply_rope(k_l, cs), v_l) @ w_o
    return y_c, y_l, k_c, v_c


def peer(h, w_q, q_gain, sub_keys, u_tab, v_tab):
    B_, S, D = h.shape
    xs = h.reshape(-1, BLOCK, D)

    def one_chunk(xc):
        q = rms_norm((xc @ w_q).reshape(BLOCK, P_HEADS, 2 * P_DKEY), q_gain)
        q = q.reshape(BLOCK, P_HEADS, 2, P_DKEY)
        s = jnp.einsum('thpd,hpnd->thpn', q, sub_keys).astype(jnp.float32)
        s_top, i_top = lax.top_k(s, P_TOPK)
        cand = (s_top[..., 0, :, None] + s_top[..., 1, None, :]).reshape(BLOCK, P_HEADS, P_TOPK * P_TOPK)
        cand_idx = (i_top[..., 0, :, None] * P_NKEYS + i_top[..., 1, None, :]).reshape(BLOCK, P_HEADS, P_TOPK * P_TOPK)
        best, pos = lax.top_k(cand, P_TOPK)
        idx = jnp.take_along_axis(cand_idx, pos, axis=-1)
        g = jax.nn.softmax(best, axis=-1)
        a = jax.nn.gelu(jnp.einsum('td,thkd->thk', xc, u_tab[idx]), approximate=False)
        return jnp.einsum('thk,thkd->td', (g * a).astype(v_tab.dtype), v_tab[idx])

    return lax.map(one_chunk, xs).reshape(B_, S, D)


def setup_inputs(seed: int = 0) -> dict:
    key = jax.random.key(seed)
    ks = iter(jax.random.split(key, 64))

    def nrm(shape, scale=1.0):
        return jax.random.normal(next(ks), shape, jnp.float32) * scale

    def gain(shape):
        return 1.0 + 0.1 * nrm(shape)

    D = D_MODEL
    return {
        "x_prompt": nrm((BATCH, SEQ, D)),
        "x_sample": nrm((DEC_BATCH, DEC_SEQ, D)),
        "cache_mla_latent": nrm((DEC_BATCH, N_A, PAST_LEN, A_KV_LORA + A_ROPE)),
        "cache_swa_k": nrm((DEC_BATCH, N_B, PAST_LEN, B_KV, B_HD)),
        "cache_swa_v": nrm((DEC_BATCH, N_B, PAST_LEN, B_KV, B_HD)),
        "cache_gqa_k": nrm((DEC_BATCH, N_C, PAST_LEN, C_KV, C_HD)),
        "cache_gqa_v": nrm((DEC_BATCH, N_C, PAST_LEN, C_KV, C_HD)),
        "c": nrm((DEC_BATCH, D)),
        "c_ctx": nrm((D,)),
        "w_mod": nrm((DEPTH, D, N_MOD * D), D ** -0.5),
        "b_mod": nrm((DEPTH, N_MOD * D), 0.02),
        "ln_g": gain((DEPTH, 2, D)),
        "ln_b": nrm((DEPTH, 2, D), 0.02),
        "a_w_in": nrm((N_A, D, A_Q_LORA + A_KV_LORA + A_ROPE), D ** -0.5),
        "a_q_gain": gain((N_A, A_Q_LORA)),
        "a_kv_gain": gain((N_A, A_KV_LORA)),
        "a_w_uq": nrm((N_A, A_Q_LORA, A_HEADS * (A_NOPE + A_ROPE)), A_Q_LORA ** -0.5),
        "a_w_ukv": nrm((N_A, A_KV_LORA, A_HEADS * (A_NOPE + A_V)), A_KV_LORA ** -0.5),
        "a_w_o": nrm((N_A, A_HEADS * A_V, D), BETA * (A_HEADS * A_V) ** -0.5),
        "b_w_qkv": nrm((N_B, D, (B_HEADS + 2 * B_KV) * B_HD), D ** -0.5),
        "b_sink": nrm((N_B, B_HEADS), 0.5),
        "b_w_o": nrm((N_B, B_HEADS * B_HD, D), BETA * (B_HEADS * B_HD) ** -0.5),
        "c_w_qkv": nrm((N_C, D, (C_HEADS + 2 * C_KV) * C_HD), D ** -0.5),
        "c_q_gain": gain((N_C, C_HD)),
        "c_k_gain": gain((N_C, C_HD)),
        "c_w_o": nrm((N_C, C_HEADS * C_HD, D), BETA * (C_HEADS * C_HD) ** -0.5),
        "p_w_q": nrm((DEPTH, D, P_HEADS * 2 * P_DKEY), D ** -0.5),
        "p_q_gain": gain((DEPTH, 2 * P_DKEY)),
        "p_sub_keys": nrm((DEPTH, P_HEADS, 2, P_NKEYS, P_DKEY), P_DKEY ** -0.5),
        "p_u": nrm((DEPTH, P_EXPERTS, D), D ** -0.5),
        "p_v": nrm((DEPTH, P_EXPERTS, D), BETA),
    }


def reference(x_prompt, x_sample, cache_mla_latent, cache_swa_k, cache_swa_v, cache_gqa_k, cache_gqa_v,
              c, c_ctx, w_mod, b_mod, ln_g, ln_b,
              a_w_in, a_q_gain, a_kv_gain, a_w_uq, a_w_ukv, a_w_o,
              b_w_qkv, b_sink, b_w_o,
              c_w_qkv, c_q_gain, c_k_gain, c_w_o,
              p_w_q, p_q_gain, p_sub_keys, p_u, p_v):
    s_lat = x_sample.shape[1]
    cs_a = axial_angles(s_lat, A_ROPE)
    cs_b = axial_angles(s_lat, B_HD)
    cs_c = axial_angles(s_lat, C_HD)
    cond_c = jax.nn.silu(c_ctx)[None]
    cond_l = jax.nn.silu(c)
    xc, xl = x_prompt, x_sample
    st_mla, st_swa_k, st_swa_v, st_gqa_k, st_gqa_v = [], [], [], [], []
    for i in range(DEPTH):
        j = i // N_MIXERS
        kind = i % N_MIXERS
        mc = jnp.split(cond_c @ w_mod[i] + b_mod[i], N_MOD, axis=-1)
        ml = jnp.split(cond_l @ w_mod[i] + b_mod[i], N_MOD, axis=-1)
        hc = modulate(xc, mc[0], mc[1])
        hl = modulate(xl, ml[0], ml[1])
        if kind == 0:
            yc, yl, lat = mla_mixer(hc, hl, cache_mla_latent[:, j], cs_a, a_w_in[j], a_q_gain[j],
                                    a_kv_gain[j], a_w_uq[j], a_w_ukv[j], a_w_o[j])
            st_mla.append(lat)
        elif kind == 1:
            yc, yl, kc, vc = swa_mixer(hc, hl, cache_swa_k[:, j], cache_swa_v[:, j], cs_b,
                                       b_w_qkv[j], b_sink[j], b_w_o[j])
            st_swa_k.append(kc)
            st_swa_v.append(vc)
        else:
            yc, yl, kc, vc = gqa_mixer(hc, hl, cache_gqa_k[:, j], cache_gqa_v[:, j], cs_c,
                                       c_w_qkv[j], c_q_gain[j], c_k_gain[j], c_w_o[j])
            st_gqa_k.append(kc)
            st_gqa_v.append(vc)
        xc = layer_norm(ALPHA * xc + mc[2][:, None] * yc, ln_g[i, 0], ln_b[i, 0])
        xl = layer_norm(ALPHA * xl + ml[2][:, None] * yl, ln_g[i, 0], ln_b[i, 0])
        hc = modulate(xc, mc[3], mc[4])
        hl = modulate(xl, ml[3], ml[4])
        fc = peer(hc, p_w_q[i], p_q_gain[i], p_sub_keys[i], p_u[i], p_v[i])
        fl = peer(hl, p_w_q[i], p_q_gain[i], p_sub_keys[i], p_u[i], p_v[i])
        xc = layer_norm(ALPHA * xc + mc[5][:, None] * fc, ln_g[i, 1], ln_b[i, 1])
        xl = layer_norm(ALPHA * xl + ml[5][:, None] * fl, ln_g[i, 1], ln_b[i, 1])
    return (xc, xl, jnp.stack(st_mla, axis=1), jnp.stack(st_swa_k, axis=1), jnp.stack(st_swa_v, axis=1),
            jnp.stack(st_gqa_k, axis=1), jnp.stack(st_gqa_v, axis=1))
```

```python
import functools
import math

import jax
import jax.numpy as jnp
from jax import lax
from jax.experimental import pallas as pl
from jax.experimental.pallas import tpu as pltpu

F32 = jnp.float32
BF16 = jnp.bfloat16

D_MODEL = 1024
DEPTH = 4
N_MIXERS = 3
N_MOD = 6
GRID_W = 64
ROPE_THETA = 10000.0
EPS = 1e-6
ALPHA = (2 * DEPTH) ** 0.25
WINDOW = 128

A_HEADS, A_NOPE, A_ROPE, A_V, A_Q_LORA, A_KV_LORA = 8, 128, 64, 128, 512, 256
B_HEADS, B_KV, B_HD = 16, 4, 64
C_HEADS, C_KV, C_HD = 8, 4, 128
P_HEADS, P_NKEYS, P_DKEY, P_TOPK = 8, 128, 128, 16
P_EXPERTS = P_NKEYS * P_NKEYS

LANES = 128
SUBLANES = 8
VMEM_LIMIT_BYTES = 56 * 1024 * 1024

NEG_INF = float("-inf")


def _cparams(*sem):
    return pltpu.CompilerParams(dimension_semantics=sem, vmem_limit_bytes=VMEM_LIMIT_BYTES)


def _dot(a, b):
    return jnp.dot(a, b, preferred_element_type=F32)


def _dot_nt(a, b):
    return lax.dot_general(a, b, (((1,), (1,)), ((), ())), preferred_element_type=F32)


def _split_bf16(x):
    hi = x.astype(BF16)
    lo = (x - hi.astype(F32)).astype(BF16)
    return hi, lo


def _mod_kernel(cond_ref, w_ref, b_ref, o_ref):
    c = cond_ref[...]
    a = c * (1.0 / (1.0 + jnp.exp(-c)))
    a_hi, a_lo = _split_bf16(a)
    w_hi, w_lo = _split_bf16(w_ref[0])
    o_ref[0] = _dot(a_hi, w_hi) + (_dot(a_lo, w_hi) + _dot(a_hi, w_lo)) + b_ref[0]


def _modulation(cond, w_mod, b_mod):
    rows = cond.shape[0]
    out = pl.pallas_call(
        _mod_kernel,
        grid=(DEPTH, N_MOD),
        in_specs=[
            pl.BlockSpec((rows, D_MODEL), lambda l, n: (0, 0)),
            pl.BlockSpec((1, D_MODEL, D_MODEL), lambda l, n: (l, 0, n)),
            pl.BlockSpec((1, 1, D_MODEL), lambda l, n: (l * N_MOD + n, 0, 0)),
        ],
        out_specs=pl.BlockSpec((1, rows, D_MODEL), lambda l, n: (l * N_MOD + n, 0, 0)),
        out_shape=jax.ShapeDtypeStruct((DEPTH * N_MOD, rows, D_MODEL), F32),
        compiler_params=_cparams("parallel", "parallel"),
        name="modulation",
    )(cond, w_mod, b_mod.reshape(DEPTH * N_MOD, 1, D_MODEL))
    return out.reshape(DEPTH, N_MOD, rows, D_MODEL).transpose(0, 2, 1, 3)


def _oddeven_merge(lo, hi, r):
    step = r * 2
    if step < hi - lo:
        yield from _oddeven_merge(lo, hi, step)
        yield from _oddeven_merge(lo + r, hi, step)
        for i in range(lo + r, hi - r, step):
            yield (i, i + r)
    else:
        yield (lo, lo + r)


def _oddeven_sort_pairs(lo, hi):
    if hi - lo >= 1:
        mid = lo + (hi - lo) // 2
        yield from _oddeven_sort_pairs(lo, mid)
        yield from _oddeven_sort_pairs(mid + 1, hi)
        yield from _oddeven_merge(lo, hi, 1)


_SORT16 = tuple(_oddeven_sort_pairs(0, P_TOPK - 1))


def _sort_desc(v):
    v = list(v)
    for i, j in _SORT16:
        hi, lo = jnp.maximum(v[i], v[j]), jnp.minimum(v[i], v[j])
        v[i], v[j] = hi, lo
    return v


def _bitonic_to_desc(v):
    v = list(v)
    stride = len(v) // 2
    while stride >= 1:
        for i in range(len(v)):
            if (i & stride) == 0:
                hi, lo = jnp.maximum(v[i], v[i + stride]), jnp.minimum(v[i], v[i + stride])
                v[i], v[i + stride] = hi, lo
        stride //= 2
    return v


def _merge_top(x, y):
    n = len(x)
    c = [x[k] if y[n - 1 - k] is None else jnp.maximum(x[k], y[n - 1 - k]) for k in range(n)]
    return _bitonic_to_desc(c)


def _peer_pre_kernel(x_ref, mod_ref, wq_ref, keys_ref, gain_ref,
                     ht_ref, e1_ref, dthr_ref, e2_ref, c2_ref, qt_ref, st_ref, top_ref, thr_ref):
    shift = mod_ref[0, 3:4, :]
    scale = mod_ref[0, 4:5, :]
    h = x_ref[...] * (1.0 + scale) + shift
    ht_ref[...] = h.T.astype(BF16)
    qt_ref[...] = _dot(wq_ref[...], ht_ref[...])
    dk2 = 2 * P_DKEY
    ngrp = P_NKEYS // SUBLANES

    def head_scores(hd, carry):
        qh = qt_ref[pl.ds(pl.multiple_of(hd * dk2, dk2), dk2), :]
        inv = lax.rsqrt(jnp.mean(qh * qh, axis=0, keepdims=True) + EPS)
        for p in range(2):
            kg = (keys_ref[hd * 2 + p] * gain_ref[p:p + 1, :]).astype(BF16)
            s = _dot(kg, qh[p * P_DKEY:(p + 1) * P_DKEY, :].astype(BF16)) * inv
            st_ref[hd * 2 + p] = s
            v = _sort_desc([s[m * SUBLANES:(m + 1) * SUBLANES, :] for m in range(ngrp)])
            for sh in (4, 2, 1):
                v = _merge_top(v, [pltpu.roll(a, sh, axis=0) for a in v])
            for k in range(P_TOPK):
                top_ref[p, k, pl.ds(hd, 1), :] = v[k][0:1, :]
        return carry

    lax.fori_loop(0, P_HEADS, head_scores, 0)

    ta = [top_ref[0, k] for k in range(P_TOPK)]
    tb = [top_ref[1, k] for k in range(P_TOPK)]
    best = [ta[0] + tb[b] for b in range(P_TOPK)]
    for a in range(1, P_TOPK // 2):
        n = P_TOPK // (a + 1)
        best = _merge_top(best, [ta[a] + tb[b] if b < n else None for b in range(P_TOPK)])
    tail = [ta[a] + tb[0] for a in range(P_TOPK // 2, P_TOPK)]
    best = _merge_top(best, tail + [None] * (P_TOPK - len(tail)))
    z = jnp.ones_like(best[0])
    for k in range(1, P_TOPK):
        z = z + jnp.exp(best[k] - best[0])
    thr_ref[0] = best[P_TOPK - 1]
    thr_ref[1] = 1.0 / z

    def head_stats(hd, carry):
        row = pl.ds(hd, 1)
        s1 = st_ref[hd * 2]
        s2 = st_ref[hd * 2 + 1]
        thr_h = thr_ref[0, row, :]
        cnt = jnp.zeros_like(s1)
        c2 = jnp.zeros_like(s2)
        for b in range(P_TOPK):
            tb_h = top_ref[1, b, row, :]
            cnt = cnt + jnp.where(s1 + tb_h >= thr_h, 1.0, 0.0)
            c2 = c2 + jnp.where(s2 >= tb_h, 1.0, 0.0)
        e1_ref[hd] = jnp.exp(s1 - top_ref[0, 0, row, :]) * thr_ref[1, row, :]
        dthr_ref[hd] = float(P_TOPK) - cnt
        e2_ref[hd] = jnp.exp(s2 - top_ref[1, 0, row, :]).astype(BF16)
        c2_ref[hd] = c2.astype(BF16)
        return carry

    lax.fori_loop(0, P_HEADS, head_stats, 0)


def _peer_pre(x, mod, row_of_tile, wq_t, keys, gain, tp):
    n = x.shape[0]
    nh, nk = P_HEADS, P_NKEYS
    return pl.pallas_call(
        _peer_pre_kernel,
        grid=(n // tp,),
        in_specs=[
            pl.BlockSpec((tp, D_MODEL), lambda t: (t, 0)),
            pl.BlockSpec((1, N_MOD, D_MODEL), lambda t: (row_of_tile(t, tp), 0, 0)),
            pl.BlockSpec(wq_t.shape, lambda t: (0, 0)),
            pl.BlockSpec(keys.shape, lambda t: (0, 0, 0)),
            pl.BlockSpec(gain.shape, lambda t: (0, 0)),
        ],
        out_specs=[
            pl.BlockSpec((D_MODEL, tp), lambda t: (0, t)),
            pl.BlockSpec((nh, nk, tp), lambda t: (0, 0, t)),
            pl.BlockSpec((nh, nk, tp), lambda t: (0, 0, t)),
            pl.BlockSpec((nh, nk, tp), lambda t: (0, 0, t)),
            pl.BlockSpec((nh, nk, tp), lambda t: (0, 0, t)),
        ],
        out_shape=[
            jax.ShapeDtypeStruct((D_MODEL, n), BF16),
            jax.ShapeDtypeStruct((nh, nk, n), F32),
            jax.ShapeDtypeStruct((nh, nk, n), F32),
            jax.ShapeDtypeStruct((nh, nk, n), BF16),
            jax.ShapeDtypeStruct((nh, nk, n), BF16),
        ],
        scratch_shapes=[
            pltpu.VMEM((nh * 2 * P_DKEY, tp), F32),
            pltpu.VMEM((2 * nh, nk, tp), F32),
            pltpu.VMEM((2, P_TOPK, nh, tp), F32),
            pltpu.VMEM((2, nh, tp), F32),
        ],
        compiler_params=_cparams("parallel"),
        name="peer_pre",
    )(x, mod, wq_t, keys, gain)


PEER_CHUNK_ROWS = 4
PEER_EC = PEER_CHUNK_ROWS * P_NKEYS
PEER_STEP_ROWS = 2 * PEER_CHUNK_ROWS
PEER_STEP_EC = 2 * PEER_EC
PEER_JG = 16
PEER_TG = 2 * LANES
PEER_IL = 2
PEER_JL = 4


def _gelu(s):
    return 0.5 * s * (1.0 + lax.erf(s * (1.0 / math.sqrt(2.0))))


def _peer_build(s_ref, w_ref, e1_ref, dthr_ref, row0, e2_ref, c2_ref):
    tm = s_ref.shape[1]
    for g in range(tm // PEER_TG):
        lanes = slice(g * PEER_TG, (g + 1) * PEER_TG)
        for ib in range(0, PEER_CHUNK_ROWS, PEER_IL):
            drow, erow = {}, {}
            for hd in range(P_HEADS):
                for il in range(PEER_IL):
                    r = row0 + ib + il
                    drow[hd, il] = jnp.broadcast_to(dthr_ref[hd, r:r + 1, lanes], (PEER_JG, PEER_TG)).astype(BF16)
                    erow[hd, il] = jnp.broadcast_to(e1_ref[hd, r:r + 1, lanes], (PEER_JG, PEER_TG)).astype(BF16)
            for jb in range(0, P_NKEYS // PEER_JG, PEER_JL):
                acc = [[None] * PEER_JL for _ in range(PEER_IL)]
                for hd in range(P_HEADS):
                    for jl in range(PEER_JL):
                        rows = slice((jb + jl) * PEER_JG, (jb + jl + 1) * PEER_JG)
                        c2 = c2_ref[hd, rows, lanes]
                        e2 = e2_ref[hd, rows, lanes]
                        for il in range(PEER_IL):
                            t = jnp.where(c2 > drow[hd, il], e2, jnp.zeros_like(e2)) * erow[hd, il]
                            acc[il][jl] = t if acc[il][jl] is None else acc[il][jl] + t
                for il in range(PEER_IL):
                    for jl in range(PEER_JL):
                        r0 = (ib + il) * P_NKEYS + (jb + jl) * PEER_JG
                        a = _gelu(s_ref[r0:r0 + PEER_JG, lanes])
                        w_ref[r0:r0 + PEER_JG, lanes] = a.astype(BF16) * acc[il][jl]


def _peer_main_kernel(u_ref, vt_ref, ht_ref, e1p_ref, dp_ref, e1c_ref, dc_ref, e2_ref, c2_ref, o_ref,
                      sa_ref, sb_ref, wa_ref, wb_ref, acc_ref):
    k = pl.program_id(1)

    @pl.when(k == 0)
    def _():
        sb_ref[...] = jnp.zeros_like(sb_ref)
        wa_ref[...] = jnp.zeros_like(wa_ref)
        acc_ref[...] = jnp.zeros_like(acc_ref)

    sa_ref[...] = _dot(u_ref[0:PEER_EC, :], ht_ref[...])
    _peer_build(sb_ref, wb_ref, e1p_ref, dp_ref, PEER_CHUNK_ROWS, e2_ref, c2_ref)
    acc_ref[...] += _dot(vt_ref[:, 0:PEER_EC], wa_ref[...])
    sb_ref[...] = _dot(u_ref[PEER_EC:PEER_STEP_EC, :], ht_ref[...])
    _peer_build(sa_ref, wa_ref, e1c_ref, dc_ref, 0, e2_ref, c2_ref)
    acc_ref[...] += _dot(vt_ref[:, PEER_EC:PEER_STEP_EC], wb_ref[...])

    @pl.when(k == pl.num_programs(1) - 1)
    def _():
        o_ref[...] = acc_ref[...].T


def _peer_main(u_b, vt_b, ht, e1, dthr, e2, c2, tm):
    n = ht.shape[1]
    nh, nk = P_HEADS, P_NKEYS
    rows = PEER_STEP_ROWS
    nsteps = P_EXPERTS // PEER_STEP_EC
    cur = lambda t, k: (0, jnp.minimum(k, nsteps - 1), t)
    prev = lambda t, k: (0, jnp.maximum(k - 1, 0), t)
    return pl.pallas_call(
        _peer_main_kernel,
        grid=(n // tm, nsteps + 1),
        in_specs=[
            pl.BlockSpec((PEER_STEP_EC, D_MODEL), lambda t, k: (jnp.minimum(k, nsteps - 1), 0)),
            pl.BlockSpec((D_MODEL, PEER_STEP_EC), lambda t, k: (0, jnp.maximum(k - 1, 0))),
            pl.BlockSpec((D_MODEL, tm), lambda t, k: (0, t)),
            pl.BlockSpec((nh, rows, tm), prev),
            pl.BlockSpec((nh, rows, tm), prev),
            pl.BlockSpec((nh, rows, tm), cur),
            pl.BlockSpec((nh, rows, tm), cur),
            pl.BlockSpec((nh, nk, tm), lambda t, k: (0, 0, t)),
            pl.BlockSpec((nh, nk, tm), lambda t, k: (0, 0, t)),
        ],
        out_specs=pl.BlockSpec((tm, D_MODEL), lambda t, k: (t, 0)),
        out_shape=jax.ShapeDtypeStruct((n, D_MODEL), F32),
        scratch_shapes=[
            pltpu.VMEM((PEER_EC, tm), F32),
            pltpu.VMEM((PEER_EC, tm), F32),
            pltpu.VMEM((PEER_EC, tm), BF16),
            pltpu.VMEM((PEER_EC, tm), BF16),
            pltpu.VMEM((D_MODEL, tm), F32),
        ],
        compiler_params=_cparams("parallel", "arbitrary"),
        name="peer_main",
    )(u_b, vt_b, ht, e1, dthr, e1, dthr, e2, c2)


def _peer(x, mod, row_of_tile, w_q, q_gain, sub_keys, u_tab, v_tab, tp, tm):
    wq_t = w_q.T.astype(BF16)
    keys = sub_keys.reshape(P_HEADS * 2, P_NKEYS, P_DKEY)
    gain = q_gain.reshape(2, P_DKEY)
    ht, e1, dthr, e2, c2 = _peer_pre(x, mod, row_of_tile, wq_t, keys, gain, tp)
    return _peer_main(u_tab.astype(BF16), v_tab.T.astype(BF16), ht, e1, dthr, e2, c2, tm)


def _rms(x, gain):
    return x * lax.rsqrt(jnp.mean(x * x, axis=-1, keepdims=True) + EPS) * gain


def _layer_norm(x, g, b):
    mu = jnp.mean(x, axis=-1, keepdims=True)
    xc = x - mu
    var = jnp.mean(xc * xc, axis=-1, keepdims=True)
    return xc * lax.rsqrt(var + EPS) * g + b


def _rope(x, c, sa, sb, shift):
    return x * c + pltpu.roll(x, shift, axis=1) * sa + pltpu.roll(x, LANES - shift, axis=1) * sb


def _rope_tables(n, rot_dim):
    n_rows = n // GRID_W
    rows = jnp.repeat(jnp.arange(n_rows, dtype=F32), GRID_W)
    cols = jnp.tile(jnp.arange(GRID_W, dtype=F32), n_rows)
    quarter = rot_dim // 4
    freqs = ROPE_THETA ** (-jnp.arange(quarter, dtype=F32) / quarter)
    ar, ac = rows[:, None] * freqs, cols[:, None] * freqs
    cr, sr, cc, sc = jnp.cos(ar), jnp.sin(ar), jnp.cos(ac), jnp.sin(ac)
    zero = jnp.zeros_like(cr)
    pad1 = jnp.ones((n, LANES - rot_dim), F32)
    pad0 = jnp.zeros((n, LANES - rot_dim), F32)
    c = jnp.concatenate([cr, cr, cc, cc, pad1], -1)
    sa = jnp.concatenate([zero, sr, zero, sc, pad0], -1)
    sb = jnp.concatenate([-sr, zero, -sc, zero, pad0], -1)
    return c, sa, sb


TOK_TILE = 512


def _tok_spec(width, off_tiles=0):
    return pl.BlockSpec((TOK_TILE, width), lambda t: (t + off_tiles, 0))


def _full_spec(shape):
    nd = len(shape)
    return pl.BlockSpec(shape, lambda t: (0,) * nd)


A_DK = 2 * LANES


def _mla_expand(kvn, krz, wukv_ref, k_ref, v_ref):
    kv = _dot(kvn.astype(BF16), wukv_ref[...])
    krz = krz.astype(BF16)
    for hd in range(A_HEADS):
        k_ref[:, hd * A_DK:hd * A_DK + A_NOPE] = kv[:, hd * A_NOPE:(hd + 1) * A_NOPE].astype(BF16)
        k_ref[:, hd * A_DK + A_NOPE:(hd + 1) * A_DK] = krz
    v_ref[...] = kv[:, A_HEADS * A_NOPE:].astype(BF16)


def _mla_proj_kernel(rotate, x_ref, mod_ref, win_ref, gq_ref, gkv_ref, wuq_ref, wukv_ref, *rest):
    if rotate:
        c_ref, sa_ref, sb_ref, q1_ref, q2_ref, k_ref, v_ref = rest
    else:
        q1_ref, k_ref, v_ref, lat_ref = rest
    h = (x_ref[...] * (1.0 + mod_ref[0, 1:2, :]) + mod_ref[0, 0:1, :]).astype(BF16)
    p = _dot(h, win_ref[...])
    cq = _rms(p[:, :A_Q_LORA], gq_ref[...]).astype(BF16)
    kvn = _rms(p[:, A_Q_LORA:A_Q_LORA + A_KV_LORA], gkv_ref[...])
    krz = p[:, A_Q_LORA + A_KV_LORA:]
    q = _dot(cq, wuq_ref[...])
    q1_ref[...] = q.astype(BF16)
    if rotate:
        c, sa, sb = c_ref[...], sa_ref[...], sb_ref[...]
        for hd in range(A_HEADS):
            lo = hd * A_DK
            q2_ref[:, lo:lo + A_NOPE] = q[:, lo:lo + A_NOPE].astype(BF16)
            q2_ref[:, lo + A_NOPE:lo + A_DK] = _rope(q[:, lo + A_NOPE:lo + A_DK], c, sa, sb, A_ROPE // 4).astype(BF16)
        krz = _rope(krz, c, sa, sb, A_ROPE // 4)
    else:
        lat_ref[:, :A_KV_LORA] = kvn
        lat_ref[:, A_KV_LORA:] = krz[:, :A_ROPE]
    _mla_expand(kvn, krz, wukv_ref, k_ref, v_ref)


def _mla_proj(x, mod, row_of_tile, off_tiles, n, w, tables):
    rotate = tables is not None
    hk = A_HEADS * A_DK
    hv = A_HEADS * A_V
    ins = [x, mod, w["w_in"], w["gq"], w["gkv"], w["w_uq"], w["w_ukv"]]
    in_specs = [
        _tok_spec(D_MODEL, off_tiles),
        pl.BlockSpec((1, N_MOD, D_MODEL), lambda t: (row_of_tile(t + off_tiles, TOK_TILE), 0, 0)),
        _full_spec(w["w_in"].shape), _full_spec(w["gq"].shape), _full_spec(w["gkv"].shape),
        _full_spec(w["w_uq"].shape), _full_spec(w["w_ukv"].shape),
    ]
    if rotate:
        per_seq = tables[0].shape[0] // TOK_TILE
        ins += list(tables)
        in_specs += [pl.BlockSpec((TOK_TILE, LANES), lambda t: (t % per_seq, 0))] * 3
        outs = [(n, hk, BF16), (n, hk, BF16), (n, hk, BF16), (n, hv, BF16)]
    else:
        outs = [(n, hk, BF16), (n, hk, BF16), (n, hv, BF16), (n, A_KV_LORA + A_ROPE, F32)]
    return pl.pallas_call(
        functools.partial(_mla_proj_kernel, rotate),
        grid=(n // TOK_TILE,),
        in_specs=in_specs,
        out_specs=[_tok_spec(wd) for _, wd, _ in outs],
        out_shape=[jax.ShapeDtypeStruct((r, wd), dt) for r, wd, dt in outs],
        compiler_params=_cparams("parallel"),
        name="mla_proj_lat" if rotate else "mla_proj_ctx",
    )(*ins)


def _mla_cache_kernel(lat_ref, wukv_ref, k_ref, v_ref):
    lat = lat_ref[...]
    _mla_expand(lat[:, :A_KV_LORA], lat[:, A_KV_LORA:], wukv_ref, k_ref, v_ref)


def _mla_cache(lat_pad, w_ukv):
    n = lat_pad.shape[0]
    return pl.pallas_call(
        _mla_cache_kernel,
        grid=(n // TOK_TILE,),
        in_specs=[_tok_spec(lat_pad.shape[1]), _full_spec(w_ukv.shape)],
        out_specs=[_tok_spec(A_HEADS * A_DK), _tok_spec(A_HEADS * A_V)],
        out_shape=[jax.ShapeDtypeStruct((n, A_HEADS * A_DK), BF16), jax.ShapeDtypeStruct((n, A_HEADS * A_V), BF16)],
        compiler_params=_cparams("parallel"),
        name="mla_cache",
    )(lat_pad, w_ukv)


def _qkv_proj_kernel(cfg, x_ref, mod_ref, w_ref, *rest):
    nq, nkv, normed, shift, rotate, state_from_proj = cfg
    rest = list(rest)
    gq_ref = gk_ref = wst_ref = None
    if normed:
        gq_ref, gk_ref = rest.pop(0), rest.pop(0)
    if not rotate and not state_from_proj:
        wst_ref = rest.pop(0)
    if rotate:
        c_ref, sa_ref, sb_ref = rest.pop(0), rest.pop(0), rest.pop(0)
        q1_ref, q2_ref, k_ref, v_ref = rest
        c, sa, sb = c_ref[...], sa_ref[...], sb_ref[...]
    else:
        q1_ref, k_ref, v_ref, ks_ref, vs_ref = rest
    h = (x_ref[...] * (1.0 + mod_ref[0, 1:2, :]) + mod_ref[0, 0:1, :]).astype(BF16)
    p = _dot(h, w_ref[...])
    for i in range(nq + nkv):
        sl = slice(i * LANES, (i + 1) * LANES)
        xh = p[:, sl]
        if normed:
            xh = _rms(xh, gq_ref[...] if i < nq else gk_ref[...])
        dsl = sl if i < nq else slice((i - nq) * LANES, (i - nq + 1) * LANES)
        if i < nq:
            q1_ref[:, dsl] = xh.astype(BF16)
            if rotate:
                q2_ref[:, dsl] = _rope(xh, c, sa, sb, shift).astype(BF16)
        else:
            if rotate:
                k_ref[:, dsl] = _rope(xh, c, sa, sb, shift).astype(BF16)
            else:
                k_ref[:, dsl] = xh.astype(BF16)
                if state_from_proj:
                    ks_ref[:, dsl] = xh
    vlo = (nq + nkv) * LANES
    v = p[:, vlo:vlo + nkv * LANES]
    v_ref[...] = v.astype(BF16)
    if not rotate:
        if state_from_proj:
            vs_ref[...] = v
        else:
            st = _dot(h, wst_ref[...])
            half = st.shape[1] // 2
            ks_ref[...] = st[:, :half]
            vs_ref[...] = st[:, half:]


def _qkv_proj(x, mod, row_of_tile, off_tiles, n, w, cfg_base, tables, name):
    nq, nkv, normed, shift, state_w = cfg_base
    rotate = tables is not None
    state_from_proj = state_w == 0
    cfg = (nq, nkv, normed, shift, rotate, state_from_proj)
    ins = [x, mod, w["w_qkv"]]
    in_specs = [
        _tok_spec(D_MODEL, off_tiles),
        pl.BlockSpec((1, N_MOD, D_MODEL), lambda t: (row_of_tile(t + off_tiles, TOK_TILE), 0, 0)),
        _full_spec(w["w_qkv"].shape),
    ]
    if normed:
        ins += [w["gq"], w["gk"]]
        in_specs += [_full_spec(w["gq"].shape), _full_spec(w["gk"].shape)]
    if not rotate and not state_from_proj:
        ins.append(w["w_state"])
        in_specs.append(_full_spec(w["w_state"].shape))
    qw, kw = nq * LANES, nkv * LANES
    if rotate:
        per_seq = tables[0].shape[0] // TOK_TILE
        ins += list(tables)
        in_specs += [pl.BlockSpec((TOK_TILE, LANES), lambda t: (t % per_seq, 0))] * 3
        outs = [(qw, BF16), (qw, BF16), (kw, BF16), (kw, BF16)]
    else:
        sw = kw if state_from_proj else state_w
        outs = [(qw, BF16), (kw, BF16), (kw, BF16), (sw, F32), (sw, F32)]
    return pl.pallas_call(
        functools.partial(_qkv_proj_kernel, cfg),
        grid=(n // TOK_TILE,),
        in_specs=in_specs,
        out_specs=[_tok_spec(wd) for wd, _ in outs],
        out_shape=[jax.ShapeDtypeStruct((n, wd), dt) for wd, dt in outs],
        compiler_params=_cparams("parallel"),
        name=name,
    )(*ins)


ATT_BK = 512
MASKED = -1e30


def _attn_kernel(cfg, *refs):
    r_heads, dk, dv, bq, scale, has_seg2, has_sink, banded = cfg
    refs = list(refs)
    sink_ref = refs.pop(0) if has_sink else None
    q1_ref, k1_ref, v1_ref = refs.pop(0), refs.pop(0), refs.pop(0)
    if has_seg2:
        q2_ref, k2_ref, v2_ref = refs.pop(0), refs.pop(0), refs.pop(0)
    o_ref = refs.pop(0)
    g = pl.program_id(1)
    qi = pl.program_id(2)
    rows = r_heads * bq

    def stack(q_ref):
        if r_heads == 1:
            return q_ref[...]
        return jnp.concatenate([q_ref[:, r * dk:(r + 1) * dk] for r in range(r_heads)], axis=0)

    s = _dot_nt(stack(q1_ref), k1_ref[...]) * scale
    m = jnp.max(s, axis=1, keepdims=True)
    if has_sink:
        sk = jnp.concatenate([jnp.full((bq, 1), sink_ref[g * r_heads + r], F32) for r in range(r_heads)], axis=0)
        m = jnp.maximum(m, sk)
    p = jnp.exp(s - m)
    l = jnp.sum(p, axis=1, keepdims=True)
    if has_sink:
        l = l + jnp.exp(sk - m)
    acc = _dot(p.astype(BF16), v1_ref[...])

    if has_seg2:
        q2 = stack(q2_ref)
        nchunk = k2_ref.shape[0] // ATT_BK
        q0 = qi * bq
        if banded:
            lo = jnp.maximum(q0 - WINDOW, 0) // ATT_BK
            hi = jnp.minimum((q0 + bq - 1 + WINDOW) // ATT_BK + 1, nchunk)
            qpos = q0 + lax.broadcasted_iota(jnp.int32, (rows, ATT_BK), 0) % bq
            kofs = lax.broadcasted_iota(jnp.int32, (rows, ATT_BK), 1)
        else:
            lo, hi = 0, nchunk

        def chunk(c, carry):
            m, l, acc = carry
            ks = pl.ds(pl.multiple_of(c * ATT_BK, ATT_BK), ATT_BK)
            s = _dot_nt(q2, k2_ref[ks, :]) * scale
            if banded:
                s = jnp.where(jnp.abs(qpos - (kofs + c * ATT_BK)) <= WINDOW, s, MASKED)
            m_new = jnp.maximum(m, jnp.max(s, axis=1, keepdims=True))
            alpha = jnp.exp(m - m_new)
            p = jnp.exp(s - m_new)
            l = alpha * l + jnp.sum(p, axis=1, keepdims=True)
            acc = alpha * acc + _dot(p.astype(BF16), v2_ref[ks, :])
            return m_new, l, acc

        m, l, acc = lax.fori_loop(lo, hi, chunk, (m, l, acc))

    out = acc * (1.0 / l)
    for r in range(r_heads):
        o_ref[:, r * dv:(r + 1) * dv] = out[r * bq:(r + 1) * bq, :].astype(o_ref.dtype)


def _attention(q1, k1, v1, seg2, sink, *, n_batch, groups, r_heads, dk, dv, bq, scale, banded, name):
    s_len = q1.shape[0] // n_batch
    t1 = k1.shape[0] // n_batch
    nq = s_len // bq
    has_seg2 = seg2 is not None
    has_sink = sink is not None
    cfg = (r_heads, dk, dv, bq, scale, has_seg2, has_sink, banded)
    q_spec = pl.BlockSpec((bq, r_heads * dk), lambda b, g, i: (b * nq + i, g))
    ins, in_specs = [], []
    if has_sink:
        ins.append(sink)
        in_specs.append(pl.BlockSpec(memory_space=pltpu.SMEM))
    ins += [q1, k1, v1]
    in_specs += [q_spec, pl.BlockSpec((t1, dk), lambda b, g, i: (b, g)), pl.BlockSpec((t1, dv), lambda b, g, i: (b, g))]
    if has_seg2:
        q2, k2, v2 = seg2
        t2 = k2.shape[0] // n_batch
        ins += [q2, k2, v2]
        in_specs += [q_spec, pl.BlockSpec((t2, dk), lambda b, g, i: (b, g)), pl.BlockSpec((t2, dv), lambda b, g, i: (b, g))]
    return pl.pallas_call(
        functools.partial(_attn_kernel, cfg),
        grid=(n_batch, groups, nq),
        in_specs=in_specs,
        out_specs=pl.BlockSpec((bq, r_heads * dv), lambda b, g, i: (b * nq + i, g)),
        out_shape=jax.ShapeDtypeStruct((q1.shape[0], groups * r_heads * dv), BF16),
        compiler_params=_cparams("parallel", "parallel", "parallel"),
        name=name,
    )(*ins)


def _post_attn_kernel(x_ref, o_ref, mod_ref, wo_ref, g_ref, b_ref, y_ref):
    y = _dot(o_ref[...], wo_ref[...])
    y_ref[...] = _layer_norm(ALPHA * x_ref[...] + mod_ref[0, 2:3, :] * y, g_ref[...], b_ref[...])


def _post_attn(x, o, mod, row_of_tile, w_o, g, b):
    n = x.shape[0]
    return pl.pallas_call(
        _post_attn_kernel,
        grid=(n // TOK_TILE,),
        in_specs=[
            _tok_spec(D_MODEL), _tok_spec(o.shape[1]),
            pl.BlockSpec((1, N_MOD, D_MODEL), lambda t: (row_of_tile(t, TOK_TILE), 0, 0)),
            _full_spec(w_o.shape), _full_spec(g.shape), _full_spec(b.shape),
        ],
        out_specs=_tok_spec(D_MODEL),
        out_shape=jax.ShapeDtypeStruct((n, D_MODEL), F32),
        compiler_params=_cparams("parallel"),
        name="post_attn",
    )(x, o, mod, w_o, g, b)


def _post_peer_kernel(x_ref, f_ref, mod_ref, g_ref, b_ref, y_ref):
    y_ref[...] = _layer_norm(ALPHA * x_ref[...] + mod_ref[0, 5:6, :] * f_ref[...], g_ref[...], b_ref[...])


def _post_peer(x, f, mod, row_of_tile, g, b):
    n = x.shape[0]
    return pl.pallas_call(
        _post_peer_kernel,
        grid=(n // TOK_TILE,),
        in_specs=[
            _tok_spec(D_MODEL), _tok_spec(D_MODEL),
            pl.BlockSpec((1, N_MOD, D_MODEL), lambda t: (row_of_tile(t, TOK_TILE), 0, 0)),
            _full_spec(g.shape), _full_spec(b.shape),
        ],
        out_specs=_tok_spec(D_MODEL),
        out_shape=jax.ShapeDtypeStruct((n, D_MODEL), F32),
        compiler_params=_cparams("parallel"),
        name="post_peer",
    )(x, f, mod, g, b)


def _pad_heads(w, heads, hd, axis):
    shape = w.shape[:axis] + (heads, hd) + w.shape[axis + 1:]
    w = w.reshape(shape)
    pad = [(0, 0)] * w.ndim
    pad[axis + 1] = (0, LANES - hd)
    w = jnp.pad(w, pad)
    return w.reshape(w.shape[:axis] + (heads * LANES,) + w.shape[axis + 2:])


def _mla_weights(w_in, gq, gkv, w_uq, w_ukv, w_o):
    w_uq = jnp.pad(w_uq.reshape(A_Q_LORA, A_HEADS, A_NOPE + A_ROPE), ((0, 0), (0, 0), (0, A_DK - A_NOPE - A_ROPE)))
    w_ukv = w_ukv.reshape(A_KV_LORA, A_HEADS, A_NOPE + A_V)
    w_ukv = jnp.concatenate([w_ukv[..., :A_NOPE].reshape(A_KV_LORA, -1), w_ukv[..., A_NOPE:].reshape(A_KV_LORA, -1)], 1)
    return dict(
        w_in=jnp.pad(w_in, ((0, 0), (0, LANES - A_ROPE))).astype(BF16),
        gq=gq.reshape(1, -1), gkv=gkv.reshape(1, -1),
        w_uq=w_uq.reshape(A_Q_LORA, A_HEADS * A_DK).astype(BF16),
        w_ukv=w_ukv.astype(BF16), w_o=w_o.astype(BF16))


def _swa_weights(w_qkv, w_o):
    nq, nk = B_HEADS * B_HD, B_KV * B_HD
    w = jnp.concatenate([_pad_heads(w_qkv[:, :nq], B_HEADS, B_HD, 1),
                         _pad_heads(w_qkv[:, nq:nq + nk], B_KV, B_HD, 1),
                         _pad_heads(w_qkv[:, nq + nk:], B_KV, B_HD, 1)], 1)
    return dict(w_qkv=w.astype(BF16), w_state=w_qkv[:, nq:].astype(BF16),
                w_o=_pad_heads(w_o, B_HEADS, B_HD, 0).astype(BF16))


def _gqa_weights(w_qkv, gq, gk, w_o):
    return dict(w_qkv=w_qkv.astype(BF16), gq=gq.reshape(1, -1), gk=gk.reshape(1, -1), w_o=w_o.astype(BF16))


PEER_PRE_TILE = 512
PEER_MAIN_TILE = 1024


def kernel(x_prompt, x_sample, cache_mla_latent, cache_swa_k, cache_swa_v, cache_gqa_k, cache_gqa_v, c, c_ctx, w_mod, b_mod, ln_g, ln_b, a_w_in, a_q_gain, a_kv_gain, a_w_uq, a_w_ukv, a_w_o, b_w_qkv, b_sink, b_w_o, c_w_qkv, c_q_gain, c_k_gain, c_w_o, p_w_q, p_q_gain, p_sub_keys, p_u, p_v):
    batch, seq, d = x_prompt.shape
    dec_batch, dec_seq, _ = x_sample.shape
    past = cache_mla_latent.shape[2]
    nc, nl = batch * seq, dec_batch * dec_seq
    ctx_tiles = nc // TOK_TILE

    def row_of_tile(t, tile):
        start = t * tile
        return jnp.where(start < nc, 0, 1 + (start - nc) // dec_seq)

    cond = jnp.concatenate([c_ctx[None], c, jnp.zeros((2 * SUBLANES - 1 - dec_batch, d), F32)], 0)
    mods = _modulation(cond, w_mod, b_mod)
    x = jnp.concatenate([x_prompt.reshape(nc, d), x_sample.reshape(nl, d)], 0)
    rope64 = _rope_tables(dec_seq, A_ROPE)
    rope128 = _rope_tables(dec_seq, C_HD)

    st_mla, st_swa_k, st_swa_v, st_gqa_k, st_gqa_v = [], [], [], [], []
    for i in range(DEPTH):
        j, kind = i // N_MIXERS, i % N_MIXERS
        mod = mods[i]
        if kind == 0:
            w = _mla_weights(a_w_in[j], a_q_gain[j], a_kv_gain[j], a_w_uq[j], a_w_ukv[j], a_w_o[j])
            q_c, k_c, v_c, lat = _mla_proj(x, mod, row_of_tile, 0, nc, w, None)
            q1, q2, k_l, v_l = _mla_proj(x, mod, row_of_tile, ctx_tiles, nl, w, rope64)
            lat_cache = jnp.pad(cache_mla_latent[:, j].reshape(dec_batch * past, -1), ((0, 0), (0, LANES - A_ROPE)))
            k_cc, v_cc = _mla_cache(lat_cache, w["w_ukv"])
            att = dict(groups=A_HEADS, r_heads=1, dk=A_DK, dv=A_V, scale=(A_NOPE + A_ROPE) ** -0.5, banded=False)
            o_c = _attention(q_c, k_c, v_c, None, None, n_batch=batch, bq=seq, name="attn_a_ctx", **att)
            o_l = _attention(q1, k_cc, v_cc, (q2, k_l, v_l), None, n_batch=dec_batch, bq=512, name="attn_a_lat", **att)
            st_mla.append(lat.reshape(batch, seq, -1))
        else:
            if kind == 1:
                w = _swa_weights(b_w_qkv[j], b_w_o[j])
                cfg = (B_HEADS, B_KV, False, B_HD // 4, B_KV * B_HD)
                k_cache = _pad_heads(cache_swa_k[:, j].reshape(dec_batch * past, -1), B_KV, B_HD, 1).astype(BF16)
                v_cache = _pad_heads(cache_swa_v[:, j].reshape(dec_batch * past, -1), B_KV, B_HD, 1).astype(BF16)
                att = dict(groups=B_KV, r_heads=B_HEADS // B_KV, dk=LANES, dv=LANES, scale=B_HD ** -0.5)
                sink, tables, bq_lat, banded, nm = b_sink[j], rope64, 256, True, "b"
            else:
                w = _gqa_weights(c_w_qkv[j], c_q_gain[j], c_k_gain[j], c_w_o[j])
                cfg = (C_HEADS, C_KV, True, C_HD // 4, 0)
                k_cache = cache_gqa_k[:, j].reshape(dec_batch * past, -1).astype(BF16)
                v_cache = cache_gqa_v[:, j].reshape(dec_batch * past, -1).astype(BF16)
                att = dict(groups=C_KV, r_heads=C_HEADS // C_KV, dk=LANES, dv=LANES, scale=C_HD ** -0.5)
                sink, tables, bq_lat, banded, nm = None, rope128, 512, False, "c"
            q_c, k_c, v_c, ks, vs = _qkv_proj(x, mod, row_of_tile, 0, nc, w, cfg, None, "proj_%s_ctx" % nm)
            q1, q2, k_l, v_l = _qkv_proj(x, mod, row_of_tile, ctx_tiles, nl, w, cfg, tables, "proj_%s_lat" % nm)
            o_c = _attention(q_c, k_c, v_c, None, sink, n_batch=batch, bq=seq, banded=False,
                             name="attn_%s_ctx" % nm, **att)
            o_l = _attention(q1, k_cache, v_cache, (q2, k_l, v_l), sink, n_batch=dec_batch, bq=bq_lat, banded=banded,
                             name="attn_%s_lat" % nm, **att)
            if kind == 1:
                st_swa_k.append(ks.reshape(batch, seq, B_KV, B_HD))
                st_swa_v.append(vs.reshape(batch, seq, B_KV, B_HD))
            else:
                st_gqa_k.append(ks.reshape(batch, seq, C_KV, C_HD))
                st_gqa_v.append(vs.reshape(batch, seq, C_KV, C_HD))
        o = jnp.concatenate([o_c, o_l], 0)
        x1 = _post_attn(x, o, mod, row_of_tile, w["w_o"], ln_g[i, 0:1], ln_b[i, 0:1])
        f = _peer(x1, mod, row_of_tile, p_w_q[i], p_q_gain[i], p_sub_keys[i], p_u[i], p_v[i],
                  PEER_PRE_TILE, PEER_MAIN_TILE)
        x = _post_peer(x1, f, mod, row_of_tile, ln_g[i, 1:2], ln_b[i, 1:2])
    return (x[:nc].reshape(batch, seq, d), x[nc:].reshape(dec_batch, dec_seq, d),
            jnp.stack(st_mla, axis=1), jnp.stack(st_swa_k, axis=1), jnp.stack(st_swa_v, axis=1),
            jnp.stack(st_gqa_k, axis=1), jnp.stack(st_gqa_v, axis=1))
```

```python
import functools
import math

import jax
import jax.numpy as jnp
from jax import lax
from jax.experimental import pallas as pl
from jax.experimental.pallas import tpu as pltpu

F32 = jnp.float32
BF16 = jnp.bfloat16

D_MODEL = 1024
DEPTH = 4
N_MIXERS = 3
N_MOD = 6
GRID_W = 64
ROPE_THETA = 10000.0
EPS = 1e-6
ALPHA = (2 * DEPTH) ** 0.25
WINDOW = 128

A_HEADS, A_NOPE, A_ROPE, A_V, A_Q_LORA, A_KV_LORA = 8, 128, 64, 128, 512, 256
B_HEADS, B_KV, B_HD = 16, 4, 64
C_HEADS, C_KV, C_HD = 8, 4, 128
P_HEADS, P_NKEYS, P_DKEY, P_TOPK = 8, 128, 128, 16
P_EXPERTS = P_NKEYS * P_NKEYS

LANES = 128
SUBLANES = 8
VMEM_LIMIT_BYTES = 56 * 1024 * 1024

NEG_INF = float("-inf")


def _cparams(*sem, flags=None):
    return pltpu.CompilerParams(dimension_semantics=sem, vmem_limit_bytes=VMEM_LIMIT_BYTES, flags=flags)


def _dot(a, b):
    return jnp.dot(a, b, preferred_element_type=F32)


def _dot_nt(a, b):
    return lax.dot_general(a, b, (((1,), (1,)), ((), ())), preferred_element_type=F32)


def _split_bf16(x):
    hi = x.astype(BF16)
    lo = (x - hi.astype(F32)).astype(BF16)
    return hi, lo


def _mod_kernel(cond_ref, w_ref, b_ref, o_ref):
    c = cond_ref[...]
    a = c * (1.0 / (1.0 + jnp.exp(-c)))
    a_hi, a_lo = _split_bf16(a)
    w_hi, w_lo = _split_bf16(w_ref[0])
    o_ref[0] = _dot(a_hi, w_hi) + (_dot(a_lo, w_hi) + _dot(a_hi, w_lo)) + b_ref[0]


def _modulation(cond, w_mod, b_mod):
    rows = cond.shape[0]
    out = pl.pallas_call(
        _mod_kernel,
        grid=(DEPTH, N_MOD),
        in_specs=[
            pl.BlockSpec((rows, D_MODEL), lambda l, n: (0, 0)),
            pl.BlockSpec((1, D_MODEL, D_MODEL), lambda l, n: (l, 0, n)),
            pl.BlockSpec((1, 1, D_MODEL), lambda l, n: (l * N_MOD + n, 0, 0)),
        ],
        out_specs=pl.BlockSpec((1, rows, D_MODEL), lambda l, n: (l * N_MOD + n, 0, 0)),
        out_shape=jax.ShapeDtypeStruct((DEPTH * N_MOD, rows, D_MODEL), F32),
        compiler_params=_cparams("parallel", "parallel"),
        name="modulation",
    )(cond, w_mod, b_mod.reshape(DEPTH * N_MOD, 1, D_MODEL))
    return out.reshape(DEPTH, N_MOD, rows, D_MODEL).transpose(0, 2, 1, 3)


def _oddeven_merge(lo, hi, r):
    step = r * 2
    if step < hi - lo:
        yield from _oddeven_merge(lo, hi, step)
        yield from _oddeven_merge(lo + r, hi, step)
        for i in range(lo + r, hi - r, step):
            yield (i, i + r)
    else:
        yield (lo, lo + r)


def _oddeven_sort_pairs(lo, hi):
    if hi - lo >= 1:
        mid = lo + (hi - lo) // 2
        yield from _oddeven_sort_pairs(lo, mid)
        yield from _oddeven_sort_pairs(mid + 1, hi)
        yield from _oddeven_merge(lo, hi, 1)


_SORT16 = tuple(_oddeven_sort_pairs(0, P_TOPK - 1))


def _sort_desc(v):
    v = list(v)
    for i, j in _SORT16:
        hi, lo = jnp.maximum(v[i], v[j]), jnp.minimum(v[i], v[j])
        v[i], v[j] = hi, lo
    return v


def _bitonic_to_desc(v):
    v = list(v)
    stride = len(v) // 2
    while stride >= 1:
        for i in range(len(v)):
            if (i & stride) == 0:
                hi, lo = jnp.maximum(v[i], v[i + stride]), jnp.minimum(v[i], v[i + stride])
                v[i], v[i + stride] = hi, lo
        stride //= 2
    return v


def _merge_top(x, y):
    n = len(x)
    c = [x[k] if y[n - 1 - k] is None else jnp.maximum(x[k], y[n - 1 - k]) for k in range(n)]
    return _bitonic_to_desc(c)


def _peer_pre_kernel(x_ref, mod_ref, wq_ref, keys_ref, gain_ref,
                     ht_ref, e1_ref, dthr_ref, e2_ref, c2_ref, qt_ref, st_ref, top_ref, thr_ref):
    shift = mod_ref[0, 3:4, :]
    scale = mod_ref[0, 4:5, :]
    h = x_ref[...] * (1.0 + scale) + shift
    ht_ref[...] = h.T.astype(BF16)
    qt_ref[...] = _dot(wq_ref[...], ht_ref[...])
    dk2 = 2 * P_DKEY
    ngrp = P_NKEYS // SUBLANES

    def head_scores(hd, carry):
        qh = qt_ref[pl.ds(pl.multiple_of(hd * dk2, dk2), dk2), :]
        inv = lax.rsqrt(jnp.mean(qh * qh, axis=0, keepdims=True) + EPS)
        for p in range(2):
            kg = (keys_ref[hd * 2 + p] * gain_ref[p:p + 1, :]).astype(BF16)
            s = _dot(kg, qh[p * P_DKEY:(p + 1) * P_DKEY, :].astype(BF16)) * inv
            st_ref[hd * 2 + p] = s
            v = _sort_desc([s[m * SUBLANES:(m + 1) * SUBLANES, :] for m in range(ngrp)])
            for sh in (4, 2, 1):
                v = _merge_top(v, [pltpu.roll(a, sh, axis=0) for a in v])
            for k in range(P_TOPK):
                top_ref[p, k, pl.ds(hd, 1), :] = v[k][0:1, :]
        return carry

    lax.fori_loop(0, P_HEADS, head_scores, 0)

    ta = [top_ref[0, k] for k in range(P_TOPK)]
    tb = [top_ref[1, k] for k in range(P_TOPK)]
    best = [ta[0] + tb[b] for b in range(P_TOPK)]
    for a in range(1, P_TOPK // 2):
        n = P_TOPK // (a + 1)
        best = _merge_top(best, [ta[a] + tb[b] if b < n else None for b in range(P_TOPK)])
    tail = [ta[a] + tb[0] for a in range(P_TOPK // 2, P_TOPK)]
    best = _merge_top(best, tail + [None] * (P_TOPK - len(tail)))
    z = jnp.ones_like(best[0])
    for k in range(1, P_TOPK):
        z = z + jnp.exp(best[k] - best[0])
    thr_ref[0] = best[P_TOPK - 1]
    thr_ref[1] = 1.0 / z

    def head_stats(hd, carry):
        row = pl.ds(hd, 1)
        s1 = st_ref[hd * 2]
        s2 = st_ref[hd * 2 + 1]
        thr_h = thr_ref[0, row, :]
        dthr = jnp.full_like(s1, float(P_TOPK))
        c2 = jnp.zeros_like(s2)
        for b in range(P_TOPK):
            dthr = jnp.where(s1 + top_ref[1, b, row, :] >= thr_h, float(P_TOPK - 1 - b), dthr)
            bb = P_TOPK - 1 - b
            c2 = jnp.where(s2 >= top_ref[1, bb, row, :], float(P_TOPK - bb), c2)
        e1_ref[hd] = jnp.exp(s1 - top_ref[0, 0, row, :]) * thr_ref[1, row, :]
        dthr_ref[hd] = dthr
        e2_ref[hd] = jnp.exp(s2 - top_ref[1, 0, row, :]).astype(BF16)
        c2_ref[hd] = c2.astype(BF16)
        return carry

    lax.fori_loop(0, P_HEADS, head_stats, 0)


def _peer_pre(x, mod, row_of_tile, wq_t, keys, gain, tp):
    n = x.shape[0]
    nh, nk = P_HEADS, P_NKEYS
    return pl.pallas_call(
        _peer_pre_kernel,
        grid=(n // tp,),
        in_specs=[
            pl.BlockSpec((tp, D_MODEL), lambda t: (t, 0)),
            pl.BlockSpec((1, N_MOD, D_MODEL), lambda t: (row_of_tile(t, tp), 0, 0)),
            pl.BlockSpec(wq_t.shape, lambda t: (0, 0)),
            pl.BlockSpec(keys.shape, lambda t: (0, 0, 0)),
            pl.BlockSpec(gain.shape, lambda t: (0, 0)),
        ],
        out_specs=[
            pl.BlockSpec((D_MODEL, tp), lambda t: (0, t)),
            pl.BlockSpec((nh, nk, tp), lambda t: (0, 0, t)),
            pl.BlockSpec((nh, nk, tp), lambda t: (0, 0, t)),
            pl.BlockSpec((nh, nk, tp), lambda t: (0, 0, t)),
            pl.BlockSpec((nh, nk, tp), lambda t: (0, 0, t)),
        ],
        out_shape=[
            jax.ShapeDtypeStruct((D_MODEL, n), BF16),
            jax.ShapeDtypeStruct((nh, nk, n), F32),
            jax.ShapeDtypeStruct((nh, nk, n), F32),
            jax.ShapeDtypeStruct((nh, nk, n), BF16),
            jax.ShapeDtypeStruct((nh, nk, n), BF16),
        ],
        scratch_shapes=[
            pltpu.VMEM((nh * 2 * P_DKEY, tp), F32),
            pltpu.VMEM((2 * nh, nk, tp), F32),
            pltpu.VMEM((2, P_TOPK, nh, tp), F32),
            pltpu.VMEM((2, nh, tp), F32),
        ],
        compiler_params=_cparams("parallel"),
        name="peer_pre",
    )(x, mod, wq_t, keys, gain)


PEER_CHUNK_ROWS = 4
PEER_EC = PEER_CHUNK_ROWS * P_NKEYS
PEER_STEP_ROWS = 2 * PEER_CHUNK_ROWS
PEER_STEP_EC = 2 * PEER_EC
PEER_JG = 16
PEER_TG = 2 * LANES
PEER_IL = 2
PEER_JL = 2


PEER_MAIN_FLAGS = None


def _gelu(s):
    one = jnp.asarray(1.0, s.dtype)
    return (jnp.asarray(0.5, s.dtype) * s) * (one + lax.erf(s * jnp.asarray(1.0 / math.sqrt(2.0), s.dtype)))


PEER_LANE_BLOCK = 2 * PEER_TG


def _peer_build(a_ref, w_ref, e1_ref, dthr_ref, row0, e2_ref, c2_ref, l0):
    for g in range(PEER_LANE_BLOCK // PEER_TG):
        lanes = slice(l0 + g * PEER_TG, l0 + (g + 1) * PEER_TG)
        for ib in range(0, PEER_CHUNK_ROWS, PEER_IL):
            drow, erow = {}, {}
            for hd in range(P_HEADS):
                for il in range(PEER_IL):
                    r = row0 + ib + il
                    drow[hd, il] = jnp.broadcast_to(dthr_ref[hd, r:r + 1, lanes], (PEER_JG, PEER_TG)).astype(BF16)
                    erow[hd, il] = jnp.broadcast_to(e1_ref[hd, r:r + 1, lanes], (PEER_JG, PEER_TG)).astype(BF16)
            for jb in range(0, P_NKEYS // PEER_JG, PEER_JL):
                acc = [[None] * PEER_JL for _ in range(PEER_IL)]
                for hd in range(P_HEADS):
                    for jl in range(PEER_JL):
                        rows = slice((jb + jl) * PEER_JG, (jb + jl + 1) * PEER_JG)
                        c2 = c2_ref[hd, rows, lanes]
                        e2 = e2_ref[hd, rows, lanes]
                        for il in range(PEER_IL):
                            t = jnp.where(c2 > drow[hd, il], e2, jnp.zeros_like(e2)) * erow[hd, il]
                            acc[il][jl] = t if acc[il][jl] is None else acc[il][jl] + t
                for il in range(PEER_IL):
                    for jl in range(PEER_JL):
                        r0 = (ib + il) * P_NKEYS + (jb + jl) * PEER_JG
                        w_ref[r0:r0 + PEER_JG, lanes] = a_ref[r0:r0 + PEER_JG, lanes] * acc[il][jl]


def _peer_phase(u, vt, ht_ref, a_new_ref, a_old_ref, w_new_ref, w_old_ref, stats, acc_ref):
    e1_ref, dthr_ref, row0, e2_ref, c2_ref = stats

    a_new_ref[...] = _gelu(_dot(u, ht_ref[...]).astype(BF16))
    for i in range(ht_ref.shape[1] // PEER_LANE_BLOCK):
        _peer_build(a_old_ref, w_new_ref, e1_ref, dthr_ref, row0, e2_ref, c2_ref, i * PEER_LANE_BLOCK)
    acc_ref[...] += _dot(vt, w_old_ref[...])


def _peer_main_kernel(u_ref, vt_ref, ht_ref, e1p_ref, dp_ref, e1c_ref, dc_ref, e2_ref, c2_ref, o_ref,
                      aa_ref, ab_ref, wa_ref, wb_ref, acc_ref):
    k = pl.program_id(1)

    @pl.when(k == 0)
    def _():
        ab_ref[...] = jnp.zeros_like(ab_ref)
        wa_ref[...] = jnp.zeros_like(wa_ref)
        acc_ref[...] = jnp.zeros_like(acc_ref)

    _peer_phase(u_ref[0:PEER_EC, :], vt_ref[:, 0:PEER_EC], ht_ref, aa_ref, ab_ref, wb_ref, wa_ref,
                (e1p_ref, dp_ref, PEER_CHUNK_ROWS, e2_ref, c2_ref), acc_ref)
    _peer_phase(u_ref[PEER_EC:PEER_STEP_EC, :], vt_ref[:, PEER_EC:PEER_STEP_EC], ht_ref, ab_ref, aa_ref, wa_ref, wb_ref,
                (e1c_ref, dc_ref, 0, e2_ref, c2_ref), acc_ref)

    @pl.when(k == pl.num_programs(1) - 1)
    def _():
        o_ref[...] = acc_ref[...].T


def _peer_main(u_b, vt_b, ht, e1, dthr, e2, c2, tm):
    n = ht.shape[1]
    nh, nk = P_HEADS, P_NKEYS
    rows = PEER_STEP_ROWS
    nsteps = P_EXPERTS // PEER_STEP_EC
    cur = lambda t, k: (0, jnp.minimum(k, nsteps - 1), t)
    prev = lambda t, k: (0, jnp.maximum(k - 1, 0), t)
    return pl.pallas_call(
        _peer_main_kernel,
        grid=(n // tm, nsteps + 1),
        in_specs=[
            pl.BlockSpec((PEER_STEP_EC, D_MODEL), lambda t, k: (jnp.minimum(k, nsteps - 1), 0)),
            pl.BlockSpec((D_MODEL, PEER_STEP_EC), lambda t, k: (0, jnp.maximum(k - 1, 0))),
            pl.BlockSpec((D_MODEL, tm), lambda t, k: (0, t)),
            pl.BlockSpec((nh, rows, tm), prev),
            pl.BlockSpec((nh, rows, tm), prev),
            pl.BlockSpec((nh, rows, tm), cur),
            pl.BlockSpec((nh, rows, tm), cur),
            pl.BlockSpec((nh, nk, tm), lambda t, k: (0, 0, t)),
            pl.BlockSpec((nh, nk, tm), lambda t, k: (0, 0, t)),
        ],
        out_specs=pl.BlockSpec((tm, D_MODEL), lambda t, k: (t, 0)),
        out_shape=jax.ShapeDtypeStruct((n, D_MODEL), F32),
        scratch_shapes=[
            pltpu.VMEM((PEER_EC, tm), BF16),
            pltpu.VMEM((PEER_EC, tm), BF16),
            pltpu.VMEM((PEER_EC, tm), BF16),
            pltpu.VMEM((PEER_EC, tm), BF16),
            pltpu.VMEM((D_MODEL, tm), F32),
        ],
        compiler_params=_cparams("parallel", "arbitrary", flags=PEER_MAIN_FLAGS),
        name="peer_main",
    )(u_b, vt_b, ht, e1, dthr, e1, dthr, e2, c2)


def _peer(x, mod, row_of_tile, w_q, q_gain, sub_keys, u_tab, v_tab, tp, tm):
    wq_t = w_q.T.astype(BF16)
    keys = sub_keys.reshape(P_HEADS * 2, P_NKEYS, P_DKEY)
    gain = q_gain.reshape(2, P_DKEY)
    ht, e1, dthr, e2, c2 = _peer_pre(x, mod, row_of_tile, wq_t, keys, gain, tp)
    return _peer_main(u_tab.astype(BF16), v_tab.T.astype(BF16), ht, e1, dthr, e2, c2, tm)


def _rms(x, gain):
    return x * lax.rsqrt(jnp.mean(x * x, axis=-1, keepdims=True) + EPS) * gain


def _layer_norm(x, g, b):
    mu = jnp.mean(x, axis=-1, keepdims=True)
    xc = x - mu
    var = jnp.mean(xc * xc, axis=-1, keepdims=True)
    return xc * lax.rsqrt(var + EPS) * g + b


def _rope(x, c, sa, sb, shift):
    return x * c + pltpu.roll(x, shift, axis=1) * sa + pltpu.roll(x, LANES - shift, axis=1) * sb


def _rope_tables(n, rot_dim):
    n_rows = n // GRID_W
    rows = jnp.repeat(jnp.arange(n_rows, dtype=F32), GRID_W)
    cols = jnp.tile(jnp.arange(GRID_W, dtype=F32), n_rows)
    quarter = rot_dim // 4
    freqs = ROPE_THETA ** (-jnp.arange(quarter, dtype=F32) / quarter)
    ar, ac = rows[:, None] * freqs, cols[:, None] * freqs
    cr, sr, cc, sc = jnp.cos(ar), jnp.sin(ar), jnp.cos(ac), jnp.sin(ac)
    zero = jnp.zeros_like(cr)
    pad1 = jnp.ones((n, LANES - rot_dim), F32)
    pad0 = jnp.zeros((n, LANES - rot_dim), F32)
    c = jnp.concatenate([cr, cr, cc, cc, pad1], -1)
    sa = jnp.concatenate([zero, sr, zero, sc, pad0], -1)
    sb = jnp.concatenate([-sr, zero, -sc, zero, pad0], -1)
    return c, sa, sb


TOK_TILE = 512


def _tok_spec(width, off_tiles=0):
    return pl.BlockSpec((TOK_TILE, width), lambda t: (t + off_tiles, 0))


def _tok_t_spec(height):
    return pl.BlockSpec((height, TOK_TILE), lambda t: (0, t))


def _full_spec(shape):
    nd = len(shape)
    return pl.BlockSpec(shape, lambda t: (0,) * nd)


A_DK = 2 * LANES
A_SCALE = (A_NOPE + A_ROPE) ** -0.5


def _mla_expand(kvn, krz, wukv_ref, k_ref, v_ref):
    kv = _dot(kvn.astype(BF16), wukv_ref[...])
    krz = krz.astype(BF16)
    for hd in range(A_HEADS):
        k_ref[:, hd * A_DK:hd * A_DK + A_NOPE] = kv[:, hd * A_NOPE:(hd + 1) * A_NOPE].astype(BF16)
        k_ref[:, hd * A_DK + A_NOPE:(hd + 1) * A_DK] = krz
    v_ref[...] = kv[:, A_HEADS * A_NOPE:].T.astype(BF16)


def _mla_proj_kernel(rotate, x_ref, mod_ref, win_ref, gq_ref, gkv_ref, wuq_ref, wukv_ref, *rest):
    if rotate:
        c_ref, sa_ref, sb_ref, q1_ref, q2_ref, k_ref, v_ref = rest
    else:
        q1_ref, k_ref, v_ref, lat_ref = rest
    h = (x_ref[...] * (1.0 + mod_ref[0, 1:2, :]) + mod_ref[0, 0:1, :]).astype(BF16)
    p = _dot(h, win_ref[...])
    cq = _rms(p[:, :A_Q_LORA], gq_ref[...]).astype(BF16)
    kvn = _rms(p[:, A_Q_LORA:A_Q_LORA + A_KV_LORA], gkv_ref[...])
    krz = p[:, A_Q_LORA + A_KV_LORA:]
    q = _dot(cq, wuq_ref[...]) * A_SCALE
    q1_ref[...] = q.astype(BF16)
    if rotate:
        c, sa, sb = c_ref[...], sa_ref[...], sb_ref[...]
        for hd in range(A_HEADS):
            lo = hd * A_DK
            q2_ref[:, lo:lo + A_NOPE] = q[:, lo:lo + A_NOPE].astype(BF16)
            q2_ref[:, lo + A_NOPE:lo + A_DK] = _rope(q[:, lo + A_NOPE:lo + A_DK], c, sa, sb, A_ROPE // 4).astype(BF16)
        krz = _rope(krz, c, sa, sb, A_ROPE // 4)
    else:
        lat_ref[:, :A_KV_LORA] = kvn
        lat_ref[:, A_KV_LORA:] = krz[:, :A_ROPE]
    _mla_expand(kvn, krz, wukv_ref, k_ref, v_ref)


def _mla_proj(x, mod, row_of_tile, off_tiles, n, w, tables):
    rotate = tables is not None
    hk = A_HEADS * A_DK
    hv = A_HEADS * A_V
    ins = [x, mod, w["w_in"], w["gq"], w["gkv"], w["w_uq"], w["w_ukv"]]
    in_specs = [
        _tok_spec(D_MODEL, off_tiles),
        pl.BlockSpec((1, N_MOD, D_MODEL), lambda t: (row_of_tile(t + off_tiles, TOK_TILE), 0, 0)),
        _full_spec(w["w_in"].shape), _full_spec(w["gq"].shape), _full_spec(w["gkv"].shape),
        _full_spec(w["w_uq"].shape), _full_spec(w["w_ukv"].shape),
    ]
    if rotate:
        per_seq = tables[0].shape[0] // TOK_TILE
        ins += list(tables)
        in_specs += [pl.BlockSpec((TOK_TILE, LANES), lambda t: (t % per_seq, 0))] * 3
        outs = [(hk, BF16, False), (hk, BF16, False), (hk, BF16, False), (hv, BF16, True)]
    else:
        outs = [(hk, BF16, False), (hk, BF16, False), (hv, BF16, True), (A_KV_LORA + A_ROPE, F32, False)]
    return pl.pallas_call(
        functools.partial(_mla_proj_kernel, rotate),
        grid=(n // TOK_TILE,),
        in_specs=in_specs,
        out_specs=[_tok_t_spec(wd) if tr else _tok_spec(wd) for wd, _, tr in outs],
        out_shape=[jax.ShapeDtypeStruct((wd, n) if tr else (n, wd), dt) for wd, dt, tr in outs],
        compiler_params=_cparams("parallel"),
        name="mla_proj_lat" if rotate else "mla_proj_ctx",
    )(*ins)


def _mla_cache_kernel(lat_ref, wukv_ref, k_ref, v_ref):
    lat = lat_ref[...]
    _mla_expand(lat[:, :A_KV_LORA], lat[:, A_KV_LORA:], wukv_ref, k_ref, v_ref)


def _mla_cache(lat_pad, w_ukv):
    n = lat_pad.shape[0]
    return pl.pallas_call(
        _mla_cache_kernel,
        grid=(n // TOK_TILE,),
        in_specs=[_tok_spec(lat_pad.shape[1]), _full_spec(w_ukv.shape)],
        out_specs=[_tok_spec(A_HEADS * A_DK), _tok_t_spec(A_HEADS * A_V)],
        out_shape=[jax.ShapeDtypeStruct((n, A_HEADS * A_DK), BF16), jax.ShapeDtypeStruct((A_HEADS * A_V, n), BF16)],
        compiler_params=_cparams("parallel"),
        name="mla_cache",
    )(lat_pad, w_ukv)


def _qkv_proj_kernel(cfg, x_ref, mod_ref, w_ref, *rest):
    nq, nkv, normed, shift, scale, rotate, state_from_proj = cfg
    rest = list(rest)
    gq_ref = gk_ref = wst_ref = None
    if normed:
        gq_ref, gk_ref = rest.pop(0), rest.pop(0)
    if not rotate and not state_from_proj:
        wst_ref = rest.pop(0)
    if rotate:
        c_ref, sa_ref, sb_ref = rest.pop(0), rest.pop(0), rest.pop(0)
        q1_ref, q2_ref, k_ref, v_ref = rest
        c, sa, sb = c_ref[...], sa_ref[...], sb_ref[...]
    else:
        q1_ref, k_ref, v_ref, ks_ref, vs_ref = rest
    h = (x_ref[...] * (1.0 + mod_ref[0, 1:2, :]) + mod_ref[0, 0:1, :]).astype(BF16)
    p = _dot(h, w_ref[...])
    for i in range(nq + nkv):
        sl = slice(i * LANES, (i + 1) * LANES)
        xh = p[:, sl]
        if normed:
            xh = _rms(xh, gq_ref[...] if i < nq else gk_ref[...])
        dsl = sl if i < nq else slice((i - nq) * LANES, (i - nq + 1) * LANES)
        if i < nq:
            xh = xh * scale
            q1_ref[:, dsl] = xh.astype(BF16)
            if rotate:
                q2_ref[:, dsl] = _rope(xh, c, sa, sb, shift).astype(BF16)
        else:
            if rotate:
                k_ref[:, dsl] = _rope(xh, c, sa, sb, shift).astype(BF16)
            else:
                k_ref[:, dsl] = xh.astype(BF16)
                if state_from_proj:
                    ks_ref[:, dsl] = xh
    vlo = (nq + nkv) * LANES
    v = p[:, vlo:vlo + nkv * LANES]
    v_ref[...] = v.T.astype(BF16)
    if not rotate:
        if state_from_proj:
            vs_ref[...] = v
        else:
            st = _dot(h, wst_ref[...])
            half = st.shape[1] // 2
            ks_ref[...] = st[:, :half]
            vs_ref[...] = st[:, half:]


def _qkv_proj(x, mod, row_of_tile, off_tiles, n, w, cfg_base, tables, name):
    nq, nkv, normed, shift, scale, state_w = cfg_base
    rotate = tables is not None
    state_from_proj = state_w == 0
    cfg = (nq, nkv, normed, shift, scale, rotate, state_from_proj)
    ins = [x, mod, w["w_qkv"]]
    in_specs = [
        _tok_spec(D_MODEL, off_tiles),
        pl.BlockSpec((1, N_MOD, D_MODEL), lambda t: (row_of_tile(t + off_tiles, TOK_TILE), 0, 0)),
        _full_spec(w["w_qkv"].shape),
    ]
    if normed:
        ins += [w["gq"], w["gk"]]
        in_specs += [_full_spec(w["gq"].shape), _full_spec(w["gk"].shape)]
    if not rotate and not state_from_proj:
        ins.append(w["w_state"])
        in_specs.append(_full_spec(w["w_state"].shape))
    qw, kw = nq * LANES, nkv * LANES
    if rotate:
        per_seq = tables[0].shape[0] // TOK_TILE
        ins += list(tables)
        in_specs += [pl.BlockSpec((TOK_TILE, LANES), lambda t: (t % per_seq, 0))] * 3
        outs = [(qw, BF16, False), (qw, BF16, False), (kw, BF16, False), (kw, BF16, True)]
    else:
        sw = kw if state_from_proj else state_w
        outs = [(qw, BF16, False), (kw, BF16, False), (kw, BF16, True), (sw, F32, False), (sw, F32, False)]
    return pl.pallas_call(
        functools.partial(_qkv_proj_kernel, cfg),
        grid=(n // TOK_TILE,),
        in_specs=in_specs,
        out_specs=[_tok_t_spec(wd) if tr else _tok_spec(wd) for wd, _, tr in outs],
        out_shape=[jax.ShapeDtypeStruct((wd, n) if tr else (n, wd), dt) for wd, dt, tr in outs],
        compiler_params=_cparams("parallel"),
        name=name,
    )(*ins)


MASKED = -1e30
ATT_UNROLL = 4


def _attn_kernel(cfg, *refs):
    r_heads, dk, dv, bq, bk, has_seg2, has_sink, banded = cfg
    refs = list(refs)
    sink_ref = refs.pop(0) if has_sink else None
    bias_ref = refs.pop(0) if banded else None
    q1_ref, k1_ref, v1_ref = refs.pop(0), refs.pop(0), refs.pop(0)
    if has_seg2:
        q2_ref, k2_ref, v2_ref = refs.pop(0), refs.pop(0), refs.pop(0)
    o_ref = refs.pop(0)
    g = pl.program_id(1)
    qi = pl.program_id(2)

    def stack(q_ref):
        if r_heads == 1:
            return q_ref[...]
        return jnp.concatenate([q_ref[:, r * dk:(r + 1) * dk] for r in range(r_heads)], axis=0)

    s = _dot_nt(k1_ref[...], stack(q1_ref))
    m = jnp.max(s, axis=0, keepdims=True)
    if has_sink:
        sk = jnp.concatenate([jnp.full((1, bq), sink_ref[g * r_heads + r], F32) for r in range(r_heads)], axis=1)
        m = jnp.maximum(m, sk)
    p = jnp.exp(s - m)
    l = jnp.sum(p, axis=0, keepdims=True)
    if has_sink:
        l = l + jnp.exp(sk - m)
    acc = _dot(v1_ref[...], p.astype(BF16))

    if has_seg2:
        q2 = stack(q2_ref)
        nchunk = k2_ref.shape[0] // bk

        def chunk(c, carry, bias=None):
            m, l, acc = carry
            ks = pl.ds(pl.multiple_of(c * bk, bk), bk)
            s = _dot_nt(k2_ref[ks, :], q2)
            if bias is not None:
                s = s + bias
            m_new = jnp.maximum(m, jnp.max(s, axis=0, keepdims=True))
            alpha = jnp.exp(m - m_new)
            p = jnp.exp(s - m_new)
            l = alpha * l + jnp.sum(p, axis=0, keepdims=True)
            acc = alpha * acc + _dot(v2_ref[:, ks], p.astype(BF16))
            return m_new, l, acc

        if banded:
            carry = (m, l, acc)
            for dc in (-1, 0, 1):
                c = jnp.clip(qi + dc, 0, nchunk - 1)
                blk = jnp.where(c == qi + dc, dc + 1, 3)
                carry = chunk(c, carry, bias_ref[blk])
            m, l, acc = carry
        else:
            m, l, acc = lax.fori_loop(0, nchunk, chunk, (m, l, acc), unroll=ATT_UNROLL)

    out = acc * (1.0 / l)
    for r in range(r_heads):
        o_ref[:, r * dv:(r + 1) * dv] = out[:, r * bq:(r + 1) * bq].T.astype(o_ref.dtype)


def _band_bias(bq, r_heads):
    kk = jnp.arange(bq)[:, None]
    qq = jnp.arange(bq)[None, :]
    blocks = [jnp.where(jnp.abs(dc * bq + kk - qq) <= WINDOW, 0.0, MASKED) for dc in (-1, 0, 1)]
    blocks.append(jnp.full((bq, bq), MASKED))
    return jnp.tile(jnp.stack(blocks).astype(F32), (1, 1, r_heads))


def _attention(q1, k1, v1t, seg2, sink, *, n_batch, groups, r_heads, dk, dv, bq, bk, banded, name):
    s_len = q1.shape[0] // n_batch
    t1 = k1.shape[0] // n_batch
    nq = s_len // bq
    has_seg2 = seg2 is not None
    has_sink = sink is not None
    cfg = (r_heads, dk, dv, bq, bk, has_seg2, has_sink, banded)
    q_spec = pl.BlockSpec((bq, r_heads * dk), lambda b, g, i: (b * nq + i, g))
    ins, in_specs = [], []
    if has_sink:
        ins.append(sink)
        in_specs.append(pl.BlockSpec(memory_space=pltpu.SMEM))
    if banded:
        assert bq == bk == 2 * WINDOW
        bias = _band_bias(bq, r_heads)
        ins.append(bias)
        in_specs.append(pl.BlockSpec(bias.shape, lambda b, g, i: (0, 0, 0)))
    ins += [q1, k1, v1t]
    in_specs += [q_spec, pl.BlockSpec((t1, dk), lambda b, g, i: (b, g)), pl.BlockSpec((dv, t1), lambda b, g, i: (g, b))]
    if has_seg2:
        q2, k2, v2t = seg2
        t2 = k2.shape[0] // n_batch
        ins += [q2, k2, v2t]
        in_specs += [q_spec, pl.BlockSpec((t2, dk), lambda b, g, i: (b, g)), pl.BlockSpec((dv, t2), lambda b, g, i: (g, b))]
    return pl.pallas_call(
        functools.partial(_attn_kernel, cfg),
        grid=(n_batch, groups, nq),
        in_specs=in_specs,
        out_specs=pl.BlockSpec((bq, r_heads * dv), lambda b, g, i: (b * nq + i, g)),
        out_shape=jax.ShapeDtypeStruct((q1.shape[0], groups * r_heads * dv), BF16),
        compiler_params=_cparams("parallel", "parallel", "parallel"),
        name=name,
    )(*ins)


def _post_attn_kernel(x_ref, o_ref, mod_ref, wo_ref, g_ref, b_ref, y_ref):
    y = _dot(o_ref[...], wo_ref[...])
    y_ref[...] = _layer_norm(ALPHA * x_ref[...] + mod_ref[0, 2:3, :] * y, g_ref[...], b_ref[...])


def _post_attn(x, o, mod, row_of_tile, w_o, g, b):
    n = x.shape[0]
    return pl.pallas_call(
        _post_attn_kernel,
        grid=(n // TOK_TILE,),
        in_specs=[
            _tok_spec(D_MODEL), _tok_spec(o.shape[1]),
            pl.BlockSpec((1, N_MOD, D_MODEL), lambda t: (row_of_tile(t, TOK_TILE), 0, 0)),
            _full_spec(w_o.shape), _full_spec(g.shape), _full_spec(b.shape),
        ],
        out_specs=_tok_spec(D_MODEL),
        out_shape=jax.ShapeDtypeStruct((n, D_MODEL), F32),
        compiler_params=_cparams("parallel"),
        name="post_attn",
    )(x, o, mod, w_o, g, b)


def _post_peer_kernel(x_ref, f_ref, mod_ref, g_ref, b_ref, y_ref):
    y_ref[...] = _layer_norm(ALPHA * x_ref[...] + mod_ref[0, 5:6, :] * f_ref[...], g_ref[...], b_ref[...])


def _post_peer(x, f, mod, row_of_tile, g, b):
    n = x.shape[0]
    return pl.pallas_call(
        _post_peer_kernel,
        grid=(n // TOK_TILE,),
        in_specs=[
            _tok_spec(D_MODEL), _tok_spec(D_MODEL),
            pl.BlockSpec((1, N_MOD, D_MODEL), lambda t: (row_of_tile(t, TOK_TILE), 0, 0)),
            _full_spec(g.shape), _full_spec(b.shape),
        ],
        out_specs=_tok_spec(D_MODEL),
        out_shape=jax.ShapeDtypeStruct((n, D_MODEL), F32),
        compiler_params=_cparams("parallel"),
        name="post_peer",
    )(x, f, mod, g, b)


def _pad_heads(w, heads, hd, axis):
    shape = w.shape[:axis] + (heads, hd) + w.shape[axis + 1:]
    w = w.reshape(shape)
    pad = [(0, 0)] * w.ndim
    pad[axis + 1] = (0, LANES - hd)
    w = jnp.pad(w, pad)
    return w.reshape(w.shape[:axis] + (heads * LANES,) + w.shape[axis + 2:])


def _mla_weights(w_in, gq, gkv, w_uq, w_ukv, w_o):
    w_uq = jnp.pad(w_uq.reshape(A_Q_LORA, A_HEADS, A_NOPE + A_ROPE), ((0, 0), (0, 0), (0, A_DK - A_NOPE - A_ROPE)))
    w_ukv = w_ukv.reshape(A_KV_LORA, A_HEADS, A_NOPE + A_V)
    w_ukv = jnp.concatenate([w_ukv[..., :A_NOPE].reshape(A_KV_LORA, -1), w_ukv[..., A_NOPE:].reshape(A_KV_LORA, -1)], 1)
    return dict(
        w_in=jnp.pad(w_in, ((0, 0), (0, LANES - A_ROPE))).astype(BF16),
        gq=gq.reshape(1, -1), gkv=gkv.reshape(1, -1),
        w_uq=w_uq.reshape(A_Q_LORA, A_HEADS * A_DK).astype(BF16),
        w_ukv=w_ukv.astype(BF16), w_o=w_o.astype(BF16))


def _swa_weights(w_qkv, w_o):
    nq, nk = B_HEADS * B_HD, B_KV * B_HD
    w = jnp.concatenate([_pad_heads(w_qkv[:, :nq], B_HEADS, B_HD, 1),
                         _pad_heads(w_qkv[:, nq:nq + nk], B_KV, B_HD, 1),
                         _pad_heads(w_qkv[:, nq + nk:], B_KV, B_HD, 1)], 1)
    return dict(w_qkv=w.astype(BF16), w_state=w_qkv[:, nq:].astype(BF16),
                w_o=_pad_heads(w_o, B_HEADS, B_HD, 0).astype(BF16))


def _gqa_weights(w_qkv, gq, gk, w_o):
    return dict(w_qkv=w_qkv.astype(BF16), gq=gq.reshape(1, -1), gk=gk.reshape(1, -1), w_o=w_o.astype(BF16))


PEER_PRE_TILE = 512
PEER_MAIN_TILE = 1024


def kernel(x_prompt, x_sample, cache_mla_latent, cache_swa_k, cache_swa_v, cache_gqa_k, cache_gqa_v, c, c_ctx, w_mod, b_mod, ln_g, ln_b, a_w_in, a_q_gain, a_kv_gain, a_w_uq, a_w_ukv, a_w_o, b_w_qkv, b_sink, b_w_o, c_w_qkv, c_q_gain, c_k_gain, c_w_o, p_w_q, p_q_gain, p_sub_keys, p_u, p_v):
    batch, seq, d = x_prompt.shape
    dec_batch, dec_seq, _ = x_sample.shape
    past = cache_mla_latent.shape[2]
    nc, nl = batch * seq, dec_batch * dec_seq
    ctx_tiles = nc // TOK_TILE

    def row_of_tile(t, tile):
        start = t * tile
        return jnp.where(start < nc, 0, 1 + (start - nc) // dec_seq)

    cond = jnp.concatenate([c_ctx[None], c, jnp.zeros((2 * SUBLANES - 1 - dec_batch, d), F32)], 0)
    mods = _modulation(cond, w_mod, b_mod)
    x = jnp.concatenate([x_prompt.reshape(nc, d), x_sample.reshape(nl, d)], 0)
    rope64 = _rope_tables(dec_seq, A_ROPE)
    rope128 = _rope_tables(dec_seq, C_HD)

    st_mla, st_swa_k, st_swa_v, st_gqa_k, st_gqa_v = [], [], [], [], []
    for i in range(DEPTH):
        j, kind = i // N_MIXERS, i % N_MIXERS
        mod = mods[i]
        if kind == 0:
            w = _mla_weights(a_w_in[j], a_q_gain[j], a_kv_gain[j], a_w_uq[j], a_w_ukv[j], a_w_o[j])
            q_c, k_c, v_c, lat = _mla_proj(x, mod, row_of_tile, 0, nc, w, None)
            q1, q2, k_l, v_l = _mla_proj(x, mod, row_of_tile, ctx_tiles, nl, w, rope64)
            lat_cache = jnp.pad(cache_mla_latent[:, j].reshape(dec_batch * past, -1), ((0, 0), (0, LANES - A_ROPE)))
            k_cc, v_cc = _mla_cache(lat_cache, w["w_ukv"])
            att = dict(groups=A_HEADS, r_heads=1, dk=A_DK, dv=A_V, bk=512, banded=False)
            o_c = _attention(q_c, k_c, v_c, None, None, n_batch=batch, bq=seq, name="attn_a_ctx", **att)
            o_l = _attention(q1, k_cc, v_cc, (q2, k_l, v_l), None, n_batch=dec_batch, bq=512, name="attn_a_lat", **att)
            st_mla.append(lat.reshape(batch, seq, -1))
        else:
            if kind == 1:
                w = _swa_weights(b_w_qkv[j], b_w_o[j])
                cfg = (B_HEADS, B_KV, False, B_HD // 4, B_HD ** -0.5, B_KV * B_HD)
                k_cache = _pad_heads(cache_swa_k[:, j].reshape(dec_batch * past, -1), B_KV, B_HD, 1).astype(BF16)
                v_cache = _pad_heads(cache_swa_v[:, j].reshape(dec_batch * past, -1), B_KV, B_HD, 1).T.astype(BF16)
                att = dict(groups=B_KV, r_heads=B_HEADS // B_KV, dk=LANES, dv=LANES, bk=2 * WINDOW)
                sink, tables, bq_lat, banded, nm = b_sink[j], rope64, 2 * WINDOW, True, "b"
            else:
                w = _gqa_weights(c_w_qkv[j], c_q_gain[j], c_k_gain[j], c_w_o[j])
                cfg = (C_HEADS, C_KV, True, C_HD // 4, C_HD ** -0.5, 0)
                k_cache = cache_gqa_k[:, j].reshape(dec_batch * past, -1).astype(BF16)
                v_cache = cache_gqa_v[:, j].reshape(dec_batch * past, -1).T.astype(BF16)
                att = dict(groups=C_KV, r_heads=C_HEADS // C_KV, dk=LANES, dv=LANES, bk=512)
                sink, tables, bq_lat, banded, nm = None, rope128, 512, False, "c"
            q_c, k_c, v_c, ks, vs = _qkv_proj(x, mod, row_of_tile, 0, nc, w, cfg, None, "proj_%s_ctx" % nm)
            q1, q2, k_l, v_l = _qkv_proj(x, mod, row_of_tile, ctx_tiles, nl, w, cfg, tables, "proj_%s_lat" % nm)
            o_c = _attention(q_c, k_c, v_c, None, sink, n_batch=batch, bq=seq, banded=False,
                             name="attn_%s_ctx" % nm, **att)
            o_l = _attention(q1, k_cache, v_cache, (q2, k_l, v_l), sink, n_batch=dec_batch, bq=bq_lat, banded=banded,
                             name="attn_%s_lat" % nm, **att)
            if kind == 1:
                st_swa_k.append(ks.reshape(batch, seq, B_KV, B_HD))
                st_swa_v.append(vs.reshape(batch, seq, B_KV, B_HD))
            else:
                st_gqa_k.append(ks.reshape(batch, seq, C_KV, C_HD))
                st_gqa_v.append(vs.reshape(batch, seq, C_KV, C_HD))
        o = jnp.concatenate([o_c, o_l], 0)
        x1 = _post_attn(x, o, mod, row_of_tile, w["w_o"], ln_g[i, 0:1], ln_b[i, 0:1])
        f = _peer(x1, mod, row_of_tile, p_w_q[i], p_q_gain[i], p_sub_keys[i], p_u[i], p_v[i],
                  PEER_PRE_TILE, PEER_MAIN_TILE)
        x = _post_peer(x1, f, mod, row_of_tile, ln_g[i, 1:2], ln_b[i, 1:2])
    return (x[:nc].reshape(batch, seq, d), x[nc:].reshape(dec_batch, dec_seq, d),
            jnp.stack(st_mla, axis=1), jnp.stack(st_swa_k, axis=1), jnp.stack(st_swa_v, axis=1),
            jnp.stack(st_gqa_k, axis=1), jnp.stack(st_gqa_v, axis=1))
```

```python
import functools
import math

import jax
import jax.numpy as jnp
from jax import lax
from jax.experimental import pallas as pl
from jax.experimental.pallas import tpu as pltpu

F32 = jnp.float32
BF16 = jnp.bfloat16

D_MODEL = 1024
DEPTH = 4
N_MIXERS = 3
N_MOD = 6
GRID_W = 64
ROPE_THETA = 10000.0
EPS = 1e-6
ALPHA = (2 * DEPTH) ** 0.25
WINDOW = 128

A_HEADS, A_NOPE, A_ROPE, A_V, A_Q_LORA, A_KV_LORA = 8, 128, 64, 128, 512, 256
B_HEADS, B_KV, B_HD = 16, 4, 64
C_HEADS, C_KV, C_HD = 8, 4, 128
P_HEADS, P_NKEYS, P_DKEY, P_TOPK = 8, 128, 128, 16
P_EXPERTS = P_NKEYS * P_NKEYS

LANES = 128
SUBLANES = 8
VMEM_LIMIT_BYTES = 56 * 1024 * 1024

NEG_INF = float("-inf")


def _cparams(*sem, flags=None):
    return pltpu.CompilerParams(dimension_semantics=sem, vmem_limit_bytes=VMEM_LIMIT_BYTES, flags=flags)


def _dot(a, b):
    return jnp.dot(a, b, preferred_element_type=F32)


def _dot_nt(a, b):
    return lax.dot_general(a, b, (((1,), (1,)), ((), ())), preferred_element_type=F32)


def _split_bf16(x):
    hi = x.astype(BF16)
    lo = (x - hi.astype(F32)).astype(BF16)
    return hi, lo


def _mod_kernel(cond_ref, w_ref, b_ref, o_ref):
    c = cond_ref[...]
    a = c * (1.0 / (1.0 + jnp.exp(-c)))
    a_hi, a_lo = _split_bf16(a)
    w_hi, w_lo = _split_bf16(w_ref[0])
    o_ref[0] = _dot(a_hi, w_hi) + (_dot(a_lo, w_hi) + _dot(a_hi, w_lo)) + b_ref[0]


def _modulation(cond, w_mod, b_mod):
    rows = cond.shape[0]
    out = pl.pallas_call(
        _mod_kernel,
        grid=(DEPTH, N_MOD),
        in_specs=[
            pl.BlockSpec((rows, D_MODEL), lambda l, n: (0, 0)),
            pl.BlockSpec((1, D_MODEL, D_MODEL), lambda l, n: (l, 0, n)),
            pl.BlockSpec((1, 1, D_MODEL), lambda l, n: (l * N_MOD + n, 0, 0)),
        ],
        out_specs=pl.BlockSpec((1, rows, D_MODEL), lambda l, n: (l * N_MOD + n, 0, 0)),
        out_shape=jax.ShapeDtypeStruct((DEPTH * N_MOD, rows, D_MODEL), F32),
        compiler_params=_cparams("parallel", "parallel"),
        name="modulation",
    )(cond, w_mod, b_mod.reshape(DEPTH * N_MOD, 1, D_MODEL))
    return out.reshape(DEPTH, N_MOD, rows, D_MODEL).transpose(0, 2, 1, 3)


def _oddeven_merge(lo, hi, r):
    step = r * 2
    if step < hi - lo:
        yield from _oddeven_merge(lo, hi, step)
        yield from _oddeven_merge(lo + r, hi, step)
        for i in range(lo + r, hi - r, step):
            yield (i, i + r)
    else:
        yield (lo, lo + r)


def _oddeven_sort_pairs(lo, hi):
    if hi - lo >= 1:
        mid = lo + (hi - lo) // 2
        yield from _oddeven_sort_pairs(lo, mid)
        yield from _oddeven_sort_pairs(mid + 1, hi)
        yield from _oddeven_merge(lo, hi, 1)


_SORT16 = tuple(_oddeven_sort_pairs(0, P_TOPK - 1))


def _sort_desc(v):
    v = list(v)
    for i, j in _SORT16:
        hi, lo = jnp.maximum(v[i], v[j]), jnp.minimum(v[i], v[j])
        v[i], v[j] = hi, lo
    return v


def _bitonic_to_desc(v):
    v = list(v)
    stride = len(v) // 2
    while stride >= 1:
        for i in range(len(v)):
            if (i & stride) == 0:
                hi, lo = jnp.maximum(v[i], v[i + stride]), jnp.minimum(v[i], v[i + stride])
                v[i], v[i + stride] = hi, lo
        stride //= 2
    return v


def _merge_top(x, y):
    n = len(x)
    c = [x[k] if y[n - 1 - k] is None else jnp.maximum(x[k], y[n - 1 - k]) for k in range(n)]
    return _bitonic_to_desc(c)


def _peer_pre_kernel(x_ref, mod_ref, wq_ref, keys_ref, gain_ref,
                     ht_ref, e1_ref, dthr_ref, e2_ref, c2_ref, qt_ref, st_ref, top_ref, thr_ref):
    shift = mod_ref[0, 3:4, :]
    scale = mod_ref[0, 4:5, :]
    h = x_ref[...] * (1.0 + scale) + shift
    ht_ref[...] = h.T.astype(BF16)
    qt_ref[...] = _dot(wq_ref[...], ht_ref[...])
    dk2 = 2 * P_DKEY
    ngrp = P_NKEYS // SUBLANES

    def head_scores(hd, carry):
        qh = qt_ref[pl.ds(pl.multiple_of(hd * dk2, dk2), dk2), :]
        inv = lax.rsqrt(jnp.mean(qh * qh, axis=0, keepdims=True) + EPS)
        for p in range(2):
            kg = (keys_ref[hd * 2 + p] * gain_ref[p:p + 1, :]).astype(BF16)
            s = _dot(kg, qh[p * P_DKEY:(p + 1) * P_DKEY, :].astype(BF16)) * inv
            st_ref[hd * 2 + p] = s
            v = _sort_desc([s[m * SUBLANES:(m + 1) * SUBLANES, :] for m in range(ngrp)])
            for sh in (4, 2, 1):
                v = _merge_top(v, [pltpu.roll(a, sh, axis=0) for a in v])
            for k in range(P_TOPK):
                top_ref[p, k, pl.ds(hd, 1), :] = v[k][0:1, :]
        return carry

    lax.fori_loop(0, P_HEADS, head_scores, 0)

    ta = [top_ref[0, k] for k in range(P_TOPK)]
    tb = [top_ref[1, k] for k in range(P_TOPK)]
    best = [ta[0] + tb[b] for b in range(P_TOPK)]
    for a in range(1, P_TOPK // 2):
        n = P_TOPK // (a + 1)
        best = _merge_top(best, [ta[a] + tb[b] if b < n else None for b in range(P_TOPK)])
    tail = [ta[a] + tb[0] for a in range(P_TOPK // 2, P_TOPK)]
    best = _merge_top(best, tail + [None] * (P_TOPK - len(tail)))
    z = jnp.ones_like(best[0])
    for k in range(1, P_TOPK):
        z = z + jnp.exp(best[k] - best[0])
    thr_ref[0] = best[P_TOPK - 1]
    thr_ref[1] = 1.0 / z

    def head_stats(hd, carry):
        row = pl.ds(hd, 1)
        s1 = st_ref[hd * 2]
        s2 = st_ref[hd * 2 + 1]
        thr_h = thr_ref[0, row, :]
        dthr = jnp.full_like(s1, float(P_TOPK))
        c2 = jnp.zeros_like(s2)
        for b in range(P_TOPK):
            dthr = jnp.where(s1 + top_ref[1, b, row, :] >= thr_h, float(P_TOPK - 1 - b), dthr)
            bb = P_TOPK - 1 - b
            c2 = jnp.where(s2 >= top_ref[1, bb, row, :], float(P_TOPK - bb), c2)
        e1_ref[hd] = jnp.exp(s1 - top_ref[0, 0, row, :]) * thr_ref[1, row, :]
        dthr_ref[hd] = dthr
        e2_ref[hd] = jnp.exp(s2 - top_ref[1, 0, row, :]).astype(BF16)
        c2_ref[hd] = c2.astype(BF16)
        return carry

    lax.fori_loop(0, P_HEADS, head_stats, 0)


def _peer_pre(x, mod, row_of_tile, wq_t, keys, gain, tp):
    n = x.shape[0]
    nh, nk = P_HEADS, P_NKEYS
    return pl.pallas_call(
        _peer_pre_kernel,
        grid=(n // tp,),
        in_specs=[
            pl.BlockSpec((tp, D_MODEL), lambda t: (t, 0)),
            pl.BlockSpec((1, N_MOD, D_MODEL), lambda t: (row_of_tile(t, tp), 0, 0)),
            pl.BlockSpec(wq_t.shape, lambda t: (0, 0)),
            pl.BlockSpec(keys.shape, lambda t: (0, 0, 0)),
            pl.BlockSpec(gain.shape, lambda t: (0, 0)),
        ],
        out_specs=[
            pl.BlockSpec((D_MODEL, tp), lambda t: (0, t)),
            pl.BlockSpec((nh, nk, tp), lambda t: (0, 0, t)),
            pl.BlockSpec((nh, nk, tp), lambda t: (0, 0, t)),
            pl.BlockSpec((nh, nk, tp), lambda t: (0, 0, t)),
            pl.BlockSpec((nh, nk, tp), lambda t: (0, 0, t)),
        ],
        out_shape=[
            jax.ShapeDtypeStruct((D_MODEL, n), BF16),
            jax.ShapeDtypeStruct((nh, nk, n), F32),
            jax.ShapeDtypeStruct((nh, nk, n), F32),
            jax.ShapeDtypeStruct((nh, nk, n), BF16),
            jax.ShapeDtypeStruct((nh, nk, n), BF16),
        ],
        scratch_shapes=[
            pltpu.VMEM((nh * 2 * P_DKEY, tp), F32),
            pltpu.VMEM((2 * nh, nk, tp), F32),
            pltpu.VMEM((2, P_TOPK, nh, tp), F32),
            pltpu.VMEM((2, nh, tp), F32),
        ],
        compiler_params=_cparams("parallel"),
        name="peer_pre",
    )(x, mod, wq_t, keys, gain)


PEER_CHUNK_ROWS = 4
PEER_EC = PEER_CHUNK_ROWS * P_NKEYS
PEER_STEP_ROWS = 2 * PEER_CHUNK_ROWS
PEER_STEP_EC = 2 * PEER_EC
PEER_JG = 16
PEER_TG = 2 * LANES
PEER_IL = 2
PEER_JL = 2


PEER_MAIN_FLAGS = None


def _gelu(s):
    one = jnp.asarray(1.0, s.dtype)
    return (jnp.asarray(0.5, s.dtype) * s) * (one + lax.erf(s * jnp.asarray(1.0 / math.sqrt(2.0), s.dtype)))


PEER_LANE_BLOCK = 2 * PEER_TG


def _peer_build(a_ref, w_ref, e1_ref, dthr_ref, row0, e2_ref, c2_ref, l0):
    for g in range(PEER_LANE_BLOCK // PEER_TG):
        lanes = slice(l0 + g * PEER_TG, l0 + (g + 1) * PEER_TG)
        for ib in range(0, PEER_CHUNK_ROWS, PEER_IL):
            drow, erow = {}, {}
            for hd in range(P_HEADS):
                for il in range(PEER_IL):
                    r = row0 + ib + il
                    drow[hd, il] = jnp.broadcast_to(dthr_ref[hd, r:r + 1, lanes], (PEER_JG, PEER_TG)).astype(BF16)
                    erow[hd, il] = jnp.broadcast_to(e1_ref[hd, r:r + 1, lanes], (PEER_JG, PEER_TG)).astype(BF16)
            for jb in range(0, P_NKEYS // PEER_JG, PEER_JL):
                acc = [[None] * PEER_JL for _ in range(PEER_IL)]
                for hd in range(P_HEADS):
                    for jl in range(PEER_JL):
                        rows = slice((jb + jl) * PEER_JG, (jb + jl + 1) * PEER_JG)
                        c2 = c2_ref[hd, rows, lanes]
                        e2 = e2_ref[hd, rows, lanes]
                        for il in range(PEER_IL):
                            t = jnp.where(c2 > drow[hd, il], e2, jnp.zeros_like(e2)) * erow[hd, il]
                            acc[il][jl] = t if acc[il][jl] is None else acc[il][jl] + t
                for il in range(PEER_IL):
                    for jl in range(PEER_JL):
                        r0 = (ib + il) * P_NKEYS + (jb + jl) * PEER_JG
                        w_ref[r0:r0 + PEER_JG, lanes] = a_ref[r0:r0 + PEER_JG, lanes] * acc[il][jl]


def _peer_phase(u, vt, ht_ref, a_new_ref, a_old_ref, w_new_ref, w_old_ref, stats, acc_ref):
    e1_ref, dthr_ref, row0, e2_ref, c2_ref = stats

    for i in range(ht_ref.shape[1] // PEER_LANE_BLOCK):
        _peer_build(a_old_ref, w_new_ref, e1_ref, dthr_ref, row0, e2_ref, c2_ref, i * PEER_LANE_BLOCK)
    a_new_ref[...] = _gelu(_dot(u, ht_ref[...]).astype(BF16))
    acc_ref[...] += _dot(vt, w_old_ref[...])


def _peer_main_kernel(u_ref, vt_ref, ht_ref, e1p_ref, dp_ref, e1c_ref, dc_ref, e2_ref, c2_ref, o_ref,
                      aa_ref, ab_ref, wa_ref, wb_ref, acc_ref):
    k = pl.program_id(1)

    @pl.when(k == 0)
    def _():
        ab_ref[...] = jnp.zeros_like(ab_ref)
        wa_ref[...] = jnp.zeros_like(wa_ref)
        acc_ref[...] = jnp.zeros_like(acc_ref)

    _peer_phase(u_ref[0:PEER_EC, :], vt_ref[:, 0:PEER_EC], ht_ref, aa_ref, ab_ref, wb_ref, wa_ref,
                (e1p_ref, dp_ref, PEER_CHUNK_ROWS, e2_ref, c2_ref), acc_ref)
    _peer_phase(u_ref[PEER_EC:PEER_STEP_EC, :], vt_ref[:, PEER_EC:PEER_STEP_EC], ht_ref, ab_ref, aa_ref, wa_ref, wb_ref,
                (e1c_ref, dc_ref, 0, e2_ref, c2_ref), acc_ref)

    @pl.when(k == pl.num_programs(1) - 1)
    def _():
        o_ref[...] = acc_ref[...].T


def _peer_main(u_b, vt_b, ht, e1, dthr, e2, c2, tm):
    n = ht.shape[1]
    nh, nk = P_HEADS, P_NKEYS
    rows = PEER_STEP_ROWS
    nsteps = P_EXPERTS // PEER_STEP_EC
    cur = lambda t, k: (0, jnp.minimum(k, nsteps - 1), t)
    prev = lambda t, k: (0, jnp.maximum(k - 1, 0), t)
    return pl.pallas_call(
        _peer_main_kernel,
        grid=(n // tm, nsteps + 1),
        in_specs=[
            pl.BlockSpec((PEER_STEP_EC, D_MODEL), lambda t, k: (jnp.minimum(k, nsteps - 1), 0)),
            pl.BlockSpec((D_MODEL, PEER_STEP_EC), lambda t, k: (0, jnp.maximum(k - 1, 0))),
            pl.BlockSpec((D_MODEL, tm), lambda t, k: (0, t)),
            pl.BlockSpec((nh, rows, tm), prev),
            pl.BlockSpec((nh, rows, tm), prev),
            pl.BlockSpec((nh, rows, tm), cur),
            pl.BlockSpec((nh, rows, tm), cur),
            pl.BlockSpec((nh, nk, tm), lambda t, k: (0, 0, t)),
            pl.BlockSpec((nh, nk, tm), lambda t, k: (0, 0, t)),
        ],
        out_specs=pl.BlockSpec((tm, D_MODEL), lambda t, k: (t, 0)),
        out_shape=jax.ShapeDtypeStruct((n, D_MODEL), F32),
        scratch_shapes=[
            pltpu.VMEM((PEER_EC, tm), BF16),
            pltpu.VMEM((PEER_EC, tm), BF16),
            pltpu.VMEM((PEER_EC, tm), BF16),
            pltpu.VMEM((PEER_EC, tm), BF16),
            pltpu.VMEM((D_MODEL, tm), F32),
        ],
        compiler_params=_cparams("parallel", "arbitrary", flags=PEER_MAIN_FLAGS),
        name="peer_main",
    )(u_b, vt_b, ht, e1, dthr, e1, dthr, e2, c2)


def _peer(x, mod, row_of_tile, w_q, q_gain, sub_keys, u_tab, v_tab, tp, tm):
    wq_t = w_q.T.astype(BF16)
    keys = sub_keys.reshape(P_HEADS * 2, P_NKEYS, P_DKEY)
    gain = q_gain.reshape(2, P_DKEY)
    ht, e1, dthr, e2, c2 = _peer_pre(x, mod, row_of_tile, wq_t, keys, gain, tp)
    return _peer_main(u_tab.astype(BF16), v_tab.T.astype(BF16), ht, e1, dthr, e2, c2, tm)


def _rms(x, gain):
    return x * lax.rsqrt(jnp.mean(x * x, axis=-1, keepdims=True) + EPS) * gain


def _layer_norm(x, g, b):
    mu = jnp.mean(x, axis=-1, keepdims=True)
    xc = x - mu
    var = jnp.mean(xc * xc, axis=-1, keepdims=True)
    return xc * lax.rsqrt(var + EPS) * g + b


def _rope(x, c, sa, sb, shift):
    return x * c + pltpu.roll(x, shift, axis=1) * sa + pltpu.roll(x, LANES - shift, axis=1) * sb


def _rope_tables(n, rot_dim):
    n_rows = n // GRID_W
    rows = jnp.repeat(jnp.arange(n_rows, dtype=F32), GRID_W)
    cols = jnp.tile(jnp.arange(GRID_W, dtype=F32), n_rows)
    quarter = rot_dim // 4
    freqs = ROPE_THETA ** (-jnp.arange(quarter, dtype=F32) / quarter)
    ar, ac = rows[:, None] * freqs, cols[:, None] * freqs
    cr, sr, cc, sc = jnp.cos(ar), jnp.sin(ar), jnp.cos(ac), jnp.sin(ac)
    zero = jnp.zeros_like(cr)
    pad1 = jnp.ones((n, LANES - rot_dim), F32)
    pad0 = jnp.zeros((n, LANES - rot_dim), F32)
    c = jnp.concatenate([cr, cr, cc, cc, pad1], -1)
    sa = jnp.concatenate([zero, sr, zero, sc, pad0], -1)
    sb = jnp.concatenate([-sr, zero, -sc, zero, pad0], -1)
    return c, sa, sb


TOK_TILE = 512


def _tok_spec(width, off_tiles=0):
    return pl.BlockSpec((TOK_TILE, width), lambda t: (t + off_tiles, 0))


def _tok_t_spec(height):
    return pl.BlockSpec((height, TOK_TILE), lambda t: (0, t))


def _full_spec(shape):
    nd = len(shape)
    return pl.BlockSpec(shape, lambda t: (0,) * nd)


A_DK = 2 * LANES
A_SCALE = (A_NOPE + A_ROPE) ** -0.5


def _mla_expand(kvn, krz, wukv_ref, k_ref, v_ref):
    kv = _dot(kvn.astype(BF16), wukv_ref[...])
    krz = krz.astype(BF16)
    for hd in range(A_HEADS):
        k_ref[:, hd * A_DK:hd * A_DK + A_NOPE] = kv[:, hd * A_NOPE:(hd + 1) * A_NOPE].astype(BF16)
        k_ref[:, hd * A_DK + A_NOPE:(hd + 1) * A_DK] = krz
    v_ref[...] = kv[:, A_HEADS * A_NOPE:].T.astype(BF16)


def _mla_proj_kernel(rotate, x_ref, mod_ref, win_ref, gq_ref, gkv_ref, wuq_ref, wukv_ref, *rest):
    if rotate:
        c_ref, sa_ref, sb_ref, q1_ref, q2_ref, k_ref, v_ref = rest
    else:
        q1_ref, k_ref, v_ref, lat_ref = rest
    h = (x_ref[...] * (1.0 + mod_ref[0, 1:2, :]) + mod_ref[0, 0:1, :]).astype(BF16)
    p = _dot(h, win_ref[...])
    cq = _rms(p[:, :A_Q_LORA], gq_ref[...]).astype(BF16)
    kvn = _rms(p[:, A_Q_LORA:A_Q_LORA + A_KV_LORA], gkv_ref[...])
    krz = p[:, A_Q_LORA + A_KV_LORA:]
    q = _dot(cq, wuq_ref[...]) * A_SCALE
    q1_ref[...] = q.astype(BF16)
    if rotate:
        c, sa, sb = c_ref[...], sa_ref[...], sb_ref[...]
        for hd in range(A_HEADS):
            lo = hd * A_DK
            q2_ref[:, lo:lo + A_NOPE] = q[:, lo:lo + A_NOPE].astype(BF16)
            q2_ref[:, lo + A_NOPE:lo + A_DK] = _rope(q[:, lo + A_NOPE:lo + A_DK], c, sa, sb, A_ROPE // 4).astype(BF16)
        krz = _rope(krz, c, sa, sb, A_ROPE // 4)
    else:
        lat_ref[:, :A_KV_LORA] = kvn
        lat_ref[:, A_KV_LORA:] = krz[:, :A_ROPE]
    _mla_expand(kvn, krz, wukv_ref, k_ref, v_ref)


def _mla_proj(x, mod, row_of_tile, off_tiles, n, w, tables):
    rotate = tables is not None
    hk = A_HEADS * A_DK
    hv = A_HEADS * A_V
    ins = [x, mod, w["w_in"], w["gq"], w["gkv"], w["w_uq"], w["w_ukv"]]
    in_specs = [
        _tok_spec(D_MODEL, off_tiles),
        pl.BlockSpec((1, N_MOD, D_MODEL), lambda t: (row_of_tile(t + off_tiles, TOK_TILE), 0, 0)),
        _full_spec(w["w_in"].shape), _full_spec(w["gq"].shape), _full_spec(w["gkv"].shape),
        _full_spec(w["w_uq"].shape), _full_spec(w["w_ukv"].shape),
    ]
    if rotate:
        per_seq = tables[0].shape[0] // TOK_TILE
        ins += list(tables)
        in_specs += [pl.BlockSpec((TOK_TILE, LANES), lambda t: (t % per_seq, 0))] * 3
        outs = [(hk, BF16, False), (hk, BF16, False), (hk, BF16, False), (hv, BF16, True)]
    else:
        outs = [(hk, BF16, False), (hk, BF16, False), (hv, BF16, True), (A_KV_LORA + A_ROPE, F32, False)]
    return pl.pallas_call(
        functools.partial(_mla_proj_kernel, rotate),
        grid=(n // TOK_TILE,),
        in_specs=in_specs,
        out_specs=[_tok_t_spec(wd) if tr else _tok_spec(wd) for wd, _, tr in outs],
        out_shape=[jax.ShapeDtypeStruct((wd, n) if tr else (n, wd), dt) for wd, dt, tr in outs],
        compiler_params=_cparams("parallel"),
        name="mla_proj_lat" if rotate else "mla_proj_ctx",
    )(*ins)


def _mla_cache_kernel(lat_ref, wukv_ref, k_ref, v_ref):
    lat = lat_ref[...]
    _mla_expand(lat[:, :A_KV_LORA], lat[:, A_KV_LORA:], wukv_ref, k_ref, v_ref)


def _mla_cache(lat_pad, w_ukv):
    n = lat_pad.shape[0]
    return pl.pallas_call(
        _mla_cache_kernel,
        grid=(n // TOK_TILE,),
        in_specs=[_tok_spec(lat_pad.shape[1]), _full_spec(w_ukv.shape)],
        out_specs=[_tok_spec(A_HEADS * A_DK), _tok_t_spec(A_HEADS * A_V)],
        out_shape=[jax.ShapeDtypeStruct((n, A_HEADS * A_DK), BF16), jax.ShapeDtypeStruct((A_HEADS * A_V, n), BF16)],
        compiler_params=_cparams("parallel"),
        name="mla_cache",
    )(lat_pad, w_ukv)


def _qkv_proj_kernel(cfg, x_ref, mod_ref, w_ref, *rest):
    nq, nkv, normed, shift, scale, rotate, state_from_proj = cfg
    rest = list(rest)
    gq_ref = gk_ref = wst_ref = None
    if normed:
        gq_ref, gk_ref = rest.pop(0), rest.pop(0)
    if not rotate and not state_from_proj:
        wst_ref = rest.pop(0)
    if rotate:
        c_ref, sa_ref, sb_ref = rest.pop(0), rest.pop(0), rest.pop(0)
        q1_ref, q2_ref, k_ref, v_ref = rest
        c, sa, sb = c_ref[...], sa_ref[...], sb_ref[...]
    else:
        q1_ref, k_ref, v_ref, ks_ref, vs_ref = rest
    h = (x_ref[...] * (1.0 + mod_ref[0, 1:2, :]) + mod_ref[0, 0:1, :]).astype(BF16)
    p = _dot(h, w_ref[...])
    for i in range(nq + nkv):
        sl = slice(i * LANES, (i + 1) * LANES)
        xh = p[:, sl]
        if normed:
            xh = _rms(xh, gq_ref[...] if i < nq else gk_ref[...])
        dsl = sl if i < nq else slice((i - nq) * LANES, (i - nq + 1) * LANES)
        if i < nq:
            xh = xh * scale
            q1_ref[:, dsl] = xh.astype(BF16)
            if rotate:
                q2_ref[:, dsl] = _rope(xh, c, sa, sb, shift).astype(BF16)
        else:
            if rotate:
                k_ref[:, dsl] = _rope(xh, c, sa, sb, shift).astype(BF16)
            else:
                k_ref[:, dsl] = xh.astype(BF16)
                if state_from_proj:
                    ks_ref[:, dsl] = xh
    vlo = (nq + nkv) * LANES
    v = p[:, vlo:vlo + nkv * LANES]
    v_ref[...] = v.T.astype(BF16)
    if not rotate:
        if state_from_proj:
            vs_ref[...] = v
        else:
            st = _dot(h, wst_ref[...])
            half = st.shape[1] // 2
            ks_ref[...] = st[:, :half]
            vs_ref[...] = st[:, half:]


def _qkv_proj(x, mod, row_of_tile, off_tiles, n, w, cfg_base, tables, name):
    nq, nkv, normed, shift, scale, state_w = cfg_base
    rotate = tables is not None
    state_from_proj = state_w == 0
    cfg = (nq, nkv, normed, shift, scale, rotate, state_from_proj)
    ins = [x, mod, w["w_qkv"]]
    in_specs = [
        _tok_spec(D_MODEL, off_tiles),
        pl.BlockSpec((1, N_MOD, D_MODEL), lambda t: (row_of_tile(t + off_tiles, TOK_TILE), 0, 0)),
        _full_spec(w["w_qkv"].shape),
    ]
    if normed:
        ins += [w["gq"], w["gk"]]
        in_specs += [_full_spec(w["gq"].shape), _full_spec(w["gk"].shape)]
    if not rotate and not state_from_proj:
        ins.append(w["w_state"])
        in_specs.append(_full_spec(w["w_state"].shape))
    qw, kw = nq * LANES, nkv * LANES
    if rotate:
        per_seq = tables[0].shape[0] // TOK_TILE
        ins += list(tables)
        in_specs += [pl.BlockSpec((TOK_TILE, LANES), lambda t: (t % per_seq, 0))] * 3
        outs = [(qw, BF16, False), (qw, BF16, False), (kw, BF16, False), (kw, BF16, True)]
    else:
        sw = kw if state_from_proj else state_w
        outs = [(qw, BF16, False), (kw, BF16, False), (kw, BF16, True), (sw, F32, False), (sw, F32, False)]
    return pl.pallas_call(
        functools.partial(_qkv_proj_kernel, cfg),
        grid=(n // TOK_TILE,),
        in_specs=in_specs,
        out_specs=[_tok_t_spec(wd) if tr else _tok_spec(wd) for wd, _, tr in outs],
        out_shape=[jax.ShapeDtypeStruct((wd, n) if tr else (n, wd), dt) for wd, dt, tr in outs],
        compiler_params=_cparams("parallel"),
        name=name,
    )(*ins)


MASKED = -1e30


def _attn_kernel(cfg, *refs):
    r_heads, dk, dv, bq, bk, has_seg2, has_sink, banded = cfg
    refs = list(refs)
    sink_ref = refs.pop(0) if has_sink else None
    bias_ref = refs.pop(0) if banded else None
    q1_ref, k1_ref, v1_ref = refs.pop(0), refs.pop(0), refs.pop(0)
    if has_seg2:
        q2_ref, k2_ref, v2_ref = refs.pop(0), refs.pop(0), refs.pop(0)
    o_ref = refs.pop(0)
    g = pl.program_id(1)
    qi = pl.program_id(2)

    def stack(q_ref):
        if r_heads == 1:
            return q_ref[...]
        return jnp.concatenate([q_ref[:, r * dk:(r + 1) * dk] for r in range(r_heads)], axis=0)

    s = _dot_nt(k1_ref[...], stack(q1_ref))
    m = jnp.max(s, axis=0, keepdims=True)
    if has_sink:
        sk = jnp.concatenate([jnp.full((1, bq), sink_ref[g * r_heads + r], F32) for r in range(r_heads)], axis=1)
        m = jnp.maximum(m, sk)
    if has_seg2 and not banded:
        s2 = _dot_nt(k2_ref[...], stack(q2_ref))
        m = jnp.maximum(m, jnp.max(s2, axis=0, keepdims=True))
    p = jnp.exp(s - m)
    l = jnp.sum(p, axis=0, keepdims=True)
    if has_sink:
        l = l + jnp.exp(sk - m)
    acc = _dot(v1_ref[...], p.astype(BF16))

    if has_seg2 and not banded:
        p2 = jnp.exp(s2 - m)
        l = l + jnp.sum(p2, axis=0, keepdims=True)
        acc = acc + _dot(v2_ref[...], p2.astype(BF16))

    if has_seg2 and banded:
        q2 = stack(q2_ref)
        nchunk = k2_ref.shape[0] // bk

        def chunk(c, carry, bias):
            m, l, acc = carry
            ks = pl.ds(pl.multiple_of(c * bk, bk), bk)
            s = _dot_nt(k2_ref[ks, :], q2) + bias
            m_new = jnp.maximum(m, jnp.max(s, axis=0, keepdims=True))
            alpha = jnp.exp(m - m_new)
            p = jnp.exp(s - m_new)
            l = alpha * l + jnp.sum(p, axis=0, keepdims=True)
            acc = alpha * acc + _dot(v2_ref[:, ks], p.astype(BF16))
            return m_new, l, acc

        carry = (m, l, acc)
        for dc in (-1, 0, 1):
            c = jnp.clip(qi + dc, 0, nchunk - 1)
            blk = jnp.where(c == qi + dc, dc + 1, 3)
            carry = chunk(c, carry, bias_ref[blk])
        m, l, acc = carry

    out = acc * (1.0 / l)
    for r in range(r_heads):
        o_ref[:, r * dv:(r + 1) * dv] = out[:, r * bq:(r + 1) * bq].T.astype(o_ref.dtype)


def _band_bias(bq, r_heads):
    kk = jnp.arange(bq)[:, None]
    qq = jnp.arange(bq)[None, :]
    blocks = [jnp.where(jnp.abs(dc * bq + kk - qq) <= WINDOW, 0.0, MASKED) for dc in (-1, 0, 1)]
    blocks.append(jnp.full((bq, bq), MASKED))
    return jnp.tile(jnp.stack(blocks).astype(F32), (1, 1, r_heads))


def _attention(q1, k1, v1t, seg2, sink, *, n_batch, groups, r_heads, dk, dv, bq, bk, banded, name):
    s_len = q1.shape[0] // n_batch
    t1 = k1.shape[0] // n_batch
    nq = s_len // bq
    has_seg2 = seg2 is not None
    has_sink = sink is not None
    cfg = (r_heads, dk, dv, bq, bk, has_seg2, has_sink, banded)
    q_spec = pl.BlockSpec((bq, r_heads * dk), lambda b, g, i: (b * nq + i, g))
    ins, in_specs = [], []
    if has_sink:
        ins.append(sink)
        in_specs.append(pl.BlockSpec(memory_space=pltpu.SMEM))
    if banded:
        assert bq == bk == 2 * WINDOW
        bias = _band_bias(bq, r_heads)
        ins.append(bias)
        in_specs.append(pl.BlockSpec(bias.shape, lambda b, g, i: (0, 0, 0)))
    ins += [q1, k1, v1t]
    in_specs += [q_spec, pl.BlockSpec((t1, dk), lambda b, g, i: (b, g)), pl.BlockSpec((dv, t1), lambda b, g, i: (g, b))]
    if has_seg2:
        q2, k2, v2t = seg2
        t2 = k2.shape[0] // n_batch
        ins += [q2, k2, v2t]
        in_specs += [q_spec, pl.BlockSpec((t2, dk), lambda b, g, i: (b, g)), pl.BlockSpec((dv, t2), lambda b, g, i: (g, b))]
    return pl.pallas_call(
        functools.partial(_attn_kernel, cfg),
        grid=(n_batch, groups, nq),
        in_specs=in_specs,
        out_specs=pl.BlockSpec((bq, r_heads * dv), lambda b, g, i: (b * nq + i, g)),
        out_shape=jax.ShapeDtypeStruct((q1.shape[0], groups * r_heads * dv), BF16),
        compiler_params=_cparams("parallel", "parallel", "parallel"),
        name=name,
    )(*ins)


def _post_attn_kernel(ctx_tiles, x_ref, oc_ref, ol_ref, mod_ref, wo_ref, g_ref, b_ref, y_ref):
    o = jnp.where(pl.program_id(0) < ctx_tiles, oc_ref[...], ol_ref[...])
    y = _dot(o, wo_ref[...])
    y_ref[...] = _layer_norm(ALPHA * x_ref[...] + mod_ref[0, 2:3, :] * y, g_ref[...], b_ref[...])


def _post_attn(x, o_c, o_l, mod, row_of_tile, w_o, g, b):
    n = x.shape[0]
    ctx_tiles = o_c.shape[0] // TOK_TILE
    width = o_c.shape[1]
    return pl.pallas_call(
        functools.partial(_post_attn_kernel, ctx_tiles),
        grid=(n // TOK_TILE,),
        in_specs=[
            _tok_spec(D_MODEL),
            pl.BlockSpec((TOK_TILE, width), lambda t: (jnp.minimum(t, ctx_tiles - 1), 0)),
            pl.BlockSpec((TOK_TILE, width), lambda t: (jnp.maximum(t - ctx_tiles, 0), 0)),
            pl.BlockSpec((1, N_MOD, D_MODEL), lambda t: (row_of_tile(t, TOK_TILE), 0, 0)),
            _full_spec(w_o.shape), _full_spec(g.shape), _full_spec(b.shape),
        ],
        out_specs=_tok_spec(D_MODEL),
        out_shape=jax.ShapeDtypeStruct((n, D_MODEL), F32),
        compiler_params=_cparams("parallel"),
        name="post_attn",
    )(x, o_c, o_l, mod, w_o, g, b)


def _post_peer_kernel(x_ref, f_ref, mod_ref, g_ref, b_ref, y_ref):
    y_ref[...] = _layer_norm(ALPHA * x_ref[...] + mod_ref[0, 5:6, :] * f_ref[...], g_ref[...], b_ref[...])


def _post_peer(x, f, mod, row_of_tile, g, b):
    n = x.shape[0]
    return pl.pallas_call(
        _post_peer_kernel,
        grid=(n // TOK_TILE,),
        in_specs=[
            _tok_spec(D_MODEL), _tok_spec(D_MODEL),
            pl.BlockSpec((1, N_MOD, D_MODEL), lambda t: (row_of_tile(t, TOK_TILE), 0, 0)),
            _full_spec(g.shape), _full_spec(b.shape),
        ],
        out_specs=_tok_spec(D_MODEL),
        out_shape=jax.ShapeDtypeStruct((n, D_MODEL), F32),
        compiler_params=_cparams("parallel"),
        name="post_peer",
    )(x, f, mod, g, b)


def _pad_heads(w, heads, hd, axis):
    shape = w.shape[:axis] + (heads, hd) + w.shape[axis + 1:]
    w = w.reshape(shape)
    pad = [(0, 0)] * w.ndim
    pad[axis + 1] = (0, LANES - hd)
    w = jnp.pad(w, pad)
    return w.reshape(w.shape[:axis] + (heads * LANES,) + w.shape[axis + 2:])


def _mla_weights(w_in, gq, gkv, w_uq, w_ukv, w_o):
    w_uq = jnp.pad(w_uq.reshape(A_Q_LORA, A_HEADS, A_NOPE + A_ROPE), ((0, 0), (0, 0), (0, A_DK - A_NOPE - A_ROPE)))
    w_ukv = w_ukv.reshape(A_KV_LORA, A_HEADS, A_NOPE + A_V)
    w_ukv = jnp.concatenate([w_ukv[..., :A_NOPE].reshape(A_KV_LORA, -1), w_ukv[..., A_NOPE:].reshape(A_KV_LORA, -1)], 1)
    return dict(
        w_in=jnp.pad(w_in, ((0, 0), (0, LANES - A_ROPE))).astype(BF16),
        gq=gq.reshape(1, -1), gkv=gkv.reshape(1, -1),
        w_uq=w_uq.reshape(A_Q_LORA, A_HEADS * A_DK).astype(BF16),
        w_ukv=w_ukv.astype(BF16), w_o=w_o.astype(BF16))


def _swa_weights(w_qkv, w_o):
    nq, nk = B_HEADS * B_HD, B_KV * B_HD
    w = jnp.concatenate([_pad_heads(w_qkv[:, :nq], B_HEADS, B_HD, 1),
                         _pad_heads(w_qkv[:, nq:nq + nk], B_KV, B_HD, 1),
                         _pad_heads(w_qkv[:, nq + nk:], B_KV, B_HD, 1)], 1)
    return dict(w_qkv=w.astype(BF16), w_state=w_qkv[:, nq:].astype(BF16),
                w_o=_pad_heads(w_o, B_HEADS, B_HD, 0).astype(BF16))


def _gqa_weights(w_qkv, gq, gk, w_o):
    return dict(w_qkv=w_qkv.astype(BF16), gq=gq.reshape(1, -1), gk=gk.reshape(1, -1), w_o=w_o.astype(BF16))


PEER_PRE_TILE = 512
PEER_MAIN_TILE = 1024


def kernel(x_prompt, x_sample, cache_mla_latent, cache_swa_k, cache_swa_v, cache_gqa_k, cache_gqa_v, c, c_ctx, w_mod, b_mod, ln_g, ln_b, a_w_in, a_q_gain, a_kv_gain, a_w_uq, a_w_ukv, a_w_o, b_w_qkv, b_sink, b_w_o, c_w_qkv, c_q_gain, c_k_gain, c_w_o, p_w_q, p_q_gain, p_sub_keys, p_u, p_v):
    batch, seq, d = x_prompt.shape
    dec_batch, dec_seq, _ = x_sample.shape
    past = cache_mla_latent.shape[2]
    nc, nl = batch * seq, dec_batch * dec_seq
    ctx_tiles = nc // TOK_TILE

    def row_of_tile(t, tile):
        start = t * tile
        return jnp.where(start < nc, 0, 1 + (start - nc) // dec_seq)

    cond = jnp.concatenate([c_ctx[None], c, jnp.zeros((2 * SUBLANES - 1 - dec_batch, d), F32)], 0)
    mods = _modulation(cond, w_mod, b_mod)
    x = jnp.concatenate([x_prompt.reshape(nc, d), x_sample.reshape(nl, d)], 0)
    rope64 = _rope_tables(dec_seq, A_ROPE)
    rope128 = _rope_tables(dec_seq, C_HD)

    st_mla, st_swa_k, st_swa_v, st_gqa_k, st_gqa_v = [], [], [], [], []
    for i in range(DEPTH):
        j, kind = i // N_MIXERS, i % N_MIXERS
        mod = mods[i]
        if kind == 0:
            w = _mla_weights(a_w_in[j], a_q_gain[j], a_kv_gain[j], a_w_uq[j], a_w_ukv[j], a_w_o[j])
            q_c, k_c, v_c, lat = _mla_proj(x, mod, row_of_tile, 0, nc, w, None)
            q1, q2, k_l, v_l = _mla_proj(x, mod, row_of_tile, ctx_tiles, nl, w, rope64)
            lat_cache = jnp.pad(cache_mla_latent[:, j].reshape(dec_batch * past, -1), ((0, 0), (0, LANES - A_ROPE)))
            k_cc, v_cc = _mla_cache(lat_cache, w["w_ukv"])
            att = dict(groups=A_HEADS, r_heads=1, dk=A_DK, dv=A_V, bk=512, banded=False)
            o_c = _attention(q_c, k_c, v_c, None, None, n_batch=batch, bq=seq, name="attn_a_ctx", **att)
            o_l = _attention(q1, k_cc, v_cc, (q2, k_l, v_l), None, n_batch=dec_batch, bq=512, name="attn_a_lat", **att)
            st_mla.append(lat.reshape(batch, seq, -1))
        else:
            if kind == 1:
                w = _swa_weights(b_w_qkv[j], b_w_o[j])
                cfg = (B_HEADS, B_KV, False, B_HD // 4, B_HD ** -0.5, B_KV * B_HD)
                k_cache = _pad_heads(cache_swa_k[:, j].reshape(dec_batch * past, -1), B_KV, B_HD, 1).astype(BF16)
                v_cache = _pad_heads(cache_swa_v[:, j].reshape(dec_batch * past, -1), B_KV, B_HD, 1).T.astype(BF16)
                att = dict(groups=B_KV, r_heads=B_HEADS // B_KV, dk=LANES, dv=LANES, bk=2 * WINDOW)
                sink, tables, bq_lat, banded, nm = b_sink[j], rope64, 2 * WINDOW, True, "b"
            else:
                w = _gqa_weights(c_w_qkv[j], c_q_gain[j], c_k_gain[j], c_w_o[j])
                cfg = (C_HEADS, C_KV, True, C_HD // 4, C_HD ** -0.5, 0)
                k_cache = cache_gqa_k[:, j].reshape(dec_batch * past, -1).astype(BF16)
                v_cache = cache_gqa_v[:, j].reshape(dec_batch * past, -1).T.astype(BF16)
                att = dict(groups=C_KV, r_heads=C_HEADS // C_KV, dk=LANES, dv=LANES, bk=512)
                sink, tables, bq_lat, banded, nm = None, rope128, 256, False, "c"
            q_c, k_c, v_c, ks, vs = _qkv_proj(x, mod, row_of_tile, 0, nc, w, cfg, None, "proj_%s_ctx" % nm)
            q1, q2, k_l, v_l = _qkv_proj(x, mod, row_of_tile, ctx_tiles, nl, w, cfg, tables, "proj_%s_lat" % nm)
            o_c = _attention(q_c, k_c, v_c, None, sink, n_batch=batch, bq=seq, banded=False,
                             name="attn_%s_ctx" % nm, **att)
            o_l = _attention(q1, k_cache, v_cache, (q2, k_l, v_l), sink, n_batch=dec_batch, bq=bq_lat, banded=banded,
                             name="attn_%s_lat" % nm, **att)
            if kind == 1:
                st_swa_k.append(ks.reshape(batch, seq, B_KV, B_HD))
                st_swa_v.append(vs.reshape(batch, seq, B_KV, B_HD))
            else:
                st_gqa_k.append(ks.reshape(batch, seq, C_KV, C_HD))
                st_gqa_v.append(vs.reshape(batch, seq, C_KV, C_HD))
        x1 = _post_attn(x, o_c, o_l, mod, row_of_tile, w["w_o"], ln_g[i, 0:1], ln_b[i, 0:1])
        f = _peer(x1, mod, row_of_tile, p_w_q[i], p_q_gain[i], p_sub_keys[i], p_u[i], p_v[i],
                  PEER_PRE_TILE, PEER_MAIN_TILE)
        x = _post_peer(x1, f, mod, row_of_tile, ln_g[i, 1:2], ln_b[i, 1:2])
    return (x[:nc].reshape(batch, seq, d), x[nc:].reshape(dec_batch, dec_seq, d),
            jnp.stack(st_mla, axis=1), jnp.stack(st_swa_k, axis=1), jnp.stack(st_swa_v, axis=1),
            jnp.stack(st_gqa_k, axis=1), jnp.stack(st_gqa_v, axis=1))
```
